```python
import jax
import jax.numpy as jnp
from jax import lax
import numpy as np

D_MODEL = 1024
BATCH = 8
SEQ = 2048
DEPTH = 4
DEC_BATCH = 16
DEC_SEQ = 2048
PAST_LEN = 128

PLE_DIM = 256
N_MIXERS = 2
RET_HEADS = 4
RET_DK = D_MODEL // RET_HEADS
RET_DV = 2 * RET_DK
RET_CHUNK = 128
ATT_HEADS = 16
ATT_HEAD_DIM = D_MODEL // ATT_HEADS
ATT_GROUPS = ((128, 1), (512, 4), (2048, 16))
N_ATT_GROUPS = 3
D_FF = 2816
N_EXPERTS = 8
TOP_K = 2
D_FF_EXPERT = 3584
N_RET_LAYERS = (DEPTH + N_MIXERS - 1) // N_MIXERS
N_ATT_LAYERS = DEPTH // N_MIXERS
N_DENSE_LAYERS = (DEPTH + 1) // 2
N_MOE_LAYERS = DEPTH // 2
LN2 = 0.6931471805599453
EPS = 1e-6
NEG_BIG = -1e30

kernel_name = "hybrid_retention_dilated_moe_encoder"


def rms_norm(x, g):
    xf = x.astype(jnp.float32)
    y = xf * lax.rsqrt(jnp.mean(xf * xf, axis=-1, keepdims=True) + EPS)
    return (y * g.astype(jnp.float32)).astype(x.dtype)


def retention_direction(q, k, v, log_gamma, strict):
    b, s, h, dk = q.shape
    dv = v.shape[-1]
    c = RET_CHUNK
    nc = s // c
    idx = jnp.arange(c, dtype=jnp.float32)
    diff = idx[:, None] - idx[None, :]
    mask = (diff > 0) if strict else (diff >= 0)
    decay_intra = jnp.where(mask[None], jnp.exp(log_gamma[:, None, None] * jnp.maximum(diff, 0.0)[None]), 0.0)
    zeta = jnp.exp(log_gamma[None, :] * (c - 1 - idx)[:, None])[None, :, :, None]
    xi = jnp.exp(log_gamma[None, :] * (idx + 1)[:, None])[None, :, :, None]
    chunk_decay = jnp.exp(log_gamma * c)[None, :, None, None]

    def to_chunks(t):
        return t.reshape(b, nc, c, h, t.shape[-1]).swapaxes(0, 1)

    def step(state, inp):
        qi, ki, vi = inp
        scores = jnp.einsum('bnhk,bmhk->bhnm', qi, ki) * decay_intra[None]
        intra = jnp.einsum('bhnm,bmhv->bnhv', scores, vi)
        cross = jnp.einsum('bnhk,bhkv->bnhv', qi, state) * xi
        new_state = state * chunk_decay + jnp.einsum('bmhk,bmhv->bhkv', ki * zeta, vi)
        return new_state, intra + cross

    state0 = jnp.zeros((b, h, dk, dv), jnp.float32)
    _, ys = lax.scan(step, state0, (to_chunks(q), to_chunks(k), to_chunks(v)))
    return ys.swapaxes(0, 1).reshape(b, s, h, dv)


def retention_mixer(h, norm_g, w_in, decay_fwd, decay_bwd, gn_g, w_out):
    b, s, _ = h.shape
    proj = rms_norm(h, norm_g) @ w_in
    qk_w = RET_HEADS * RET_DK
    v_w = RET_HEADS * RET_DV
    q, k, v, g = jnp.split(proj, [qk_w, 2 * qk_w, 2 * qk_w + v_w], axis=-1)
    q = q.reshape(b, s, RET_HEADS, RET_DK).astype(jnp.float32) * (RET_DK ** -0.5)
    k = k.reshape(b, s, RET_HEADS, RET_DK).astype(jnp.float32)
    v = v.reshape(b, s, RET_HEADS, RET_DV).astype(jnp.float32)
    lg_f = jnp.log1p(-jnp.exp(decay_fwd.astype(jnp.float32)))
    lg_b = jnp.log1p(-jnp.exp(decay_bwd.astype(jnp.float32)))
    o_f = retention_direction(q, k, v, lg_f, False)
    o_b = retention_direction(q[:, ::-1], k[:, ::-1], v[:, ::-1], lg_b, True)[:, ::-1]
    o = rms_norm(o_f + o_b, gn_g)
    o = o.reshape(b, s, v_w).astype(h.dtype) * jax.nn.silu(g)
    return o @ w_out


def alibi_slopes():
    return jnp.exp2(-8.0 * jnp.arange(1, ATT_HEADS + 1, dtype=jnp.float32) / ATT_HEADS)


def dilated_group_attention(q, k, v, slopes, dilation, half):
    b, s, h, dh = q.shape
    L = s // dilation
    blk = half
    nb = -(-L // blk)
    Lp = nb * blk
    bp = b * dilation

    def to_sub(t):
        t = t.reshape(b, L, dilation, h, dh).transpose(0, 2, 1, 3, 4).reshape(bp, L, h, dh)
        return jnp.pad(t, ((0, 0), (0, Lp - L), (0, 0), (0, 0)))

    def windows(t):
        tp = jnp.pad(t, ((0, 0), (blk, blk), (0, 0), (0, 0))).reshape(bp, nb + 2, blk, h, dh)
        return jnp.concatenate([tp[:, :-2], tp[:, 1:-1], tp[:, 2:]], axis=2)

    qb = to_sub(q).reshape(bp, nb, blk, h, dh)
    kw = windows(to_sub(k))
    vw = windows(to_sub(v))
    scores = jnp.einsum('bnqhd,bnkhd->bhnqk', qb, kw).astype(jnp.float32) * (dh ** -0.5)
    qpos = jnp.arange(nb)[:, None] * blk + jnp.arange(blk)[None, :]
    kpos = jnp.arange(nb)[:, None] * blk - blk + jnp.arange(3 * blk)[None, :]
    rel = kpos[:, None, :] - qpos[:, :, None]
    valid = (jnp.abs(rel) <= half) & (kpos[:, None, :] >= 0) & (kpos[:, None, :] < L)
    bias = -slopes[:, None, None, None] * (dilation * jnp.abs(rel)).astype(jnp.float32)[None]
    scores = jnp.where(valid[None, None], scores + bias[None], NEG_BIG)
    lse = jax.nn.logsumexp(scores, axis=-1)
    probs = jnp.exp(scores - lse[..., None])
    out = jnp.einsum('bhnqk,bnkhd->bnqhd', probs.astype(vw.dtype), vw)
    out = out.reshape(bp, Lp, h, dh)[:, :L]
    out = out.reshape(b, dilation, L, h, dh).transpose(0, 2, 1, 3, 4).reshape(b, s, h, dh)
    lse = lse.transpose(0, 2, 3, 1).reshape(bp, Lp, h)[:, :L]
    lse = lse.reshape(b, dilation, L, h).transpose(0, 2, 1, 3).reshape(b, s, h)
    return out, lse


def dilated_attention_mixer(h, norm_g, w_in, q_gain, k_gain, w_out):
    b, s, _ = h.shape
    proj = (rms_norm(h, norm_g) @ w_in).reshape(b, s, 3, N_ATT_GROUPS, ATT_HEADS, ATT_HEAD_DIM)
    q = rms_norm(proj[:, :, 0], q_gain)
    k = rms_norm(proj[:, :, 1], k_gain)
    v = proj[:, :, 2]
    slopes = alibi_slopes()
    outs = []
    lses = []
    for g, (window, dilation) in enumerate(ATT_GROUPS):
        o_g, lse_g = dilated_group_attention(q[:, :, g], k[:, :, g], v[:, :, g], slopes,
                                             dilation, window // (2 * dilation))
        outs.append(o_g)
        lses.append(lse_g)
    wts = jax.nn.softmax(jnp.stack(lses, axis=0), axis=0)
    o = jnp.einsum('gbsh,gbshd->bshd', wts, jnp.stack(outs, axis=0))
    return o.reshape(b, s, ATT_HEADS * ATT_HEAD_DIM).astype(h.dtype) @ w_out


def swiglu(x, w_gate_up, w_down):
    a, u = jnp.split(x @ w_gate_up, 2, axis=-1)
    return (jax.nn.silu(a) * u) @ w_down


def dense_ffn(h, norm_g, w_gate_up, w_down):
    return swiglu(rms_norm(h, norm_g), w_gate_up, w_down)


def moe_ffn(h, norm_g, router_w, w_gate_up, w_down):
    hn = rms_norm(h, norm_g)
    logits = (hn @ router_w).astype(jnp.float32)
    top_vals, top_idx = lax.top_k(logits, TOP_K)
    top_w = jax.nn.softmax(top_vals, axis=-1)
    gates = jnp.sum(jax.nn.one_hot(top_idx, N_EXPERTS, dtype=jnp.float32) * top_w[..., None], axis=-2)
    out = jnp.zeros_like(h)
    for e in range(N_EXPERTS):
        out = out + gates[..., e:e + 1].astype(h.dtype) * swiglu(hn, w_gate_up[e], w_down[e])
    return out


def per_layer_embedding(h, p_i, norm_g, w_gate, w_proj):
    gate = jax.nn.sigmoid(rms_norm(h, norm_g) @ w_gate)
    return h + gate * (p_i @ w_proj)


def run_trunk(x, p, w):
    h = x
    for i in range(DEPTH):
        jm = i // N_MIXERS
        if i % N_MIXERS == 0:
            h = h + retention_mixer(h, w['ret_norm'][jm], w['ret_w_in'][jm], w['ret_decay_fwd'][jm],
                                    w['ret_decay_bwd'][jm], w['ret_gn'][jm], w['ret_w_out'][jm])
        else:
            h = h + dilated_attention_mixer(h, w['att_norm'][jm], w['att_w_in'][jm], w['att_q_gain'][jm],
                                            w['att_k_gain'][jm], w['att_w_out'][jm])
        jf = i // 2
        if i % 2 == 0:
            h = h + dense_ffn(h, w['ffn_norm'][jf], w['ffn_w_gate_up'][jf], w['ffn_w_down'][jf])
        else:
            h = h + moe_ffn(h, w['moe_norm'][jf], w['moe_router'][jf], w['moe_w_gate_up'][jf], w['moe_w_down'][jf])
        h = per_layer_embedding(h, p[i], w['ple_norm'][i], w['ple_w_gate'][i], w['ple_w_proj'][i])
    return h


def setup_inputs(seed: int = 0) -> dict:
    key = jax.random.key(seed)
    ks = jax.random.split(key, 32)
    f32 = jnp.float32

    def nrm(k, shape, scale):
        return jax.random.normal(k, shape, f32) * scale

    def gain(k, shape):
        return 1.0 + 0.01 * jax.random.normal(k, shape, f32)

    base_decay = (-5.0 - jnp.arange(RET_HEADS, dtype=f32)) * LN2
    ret_in = 2 * RET_HEADS * RET_DK + 2 * RET_HEADS * RET_DV
    att_in = 3 * N_ATT_GROUPS * ATT_HEADS * ATT_HEAD_DIM
    return {
        'x_prompt': nrm(ks[0], (BATCH, SEQ, D_MODEL), 1.0),
        'x_sample': nrm(ks[1], (DEC_BATCH, DEC_SEQ, D_MODEL), 1.0),
        'p_prompt': nrm(ks[2], (DEPTH, BATCH, SEQ, PLE_DIM), 1.0),
        'p_sample': nrm(ks[3], (DEPTH, DEC_BATCH, DEC_SEQ, PLE_DIM), 1.0),
        'ret_norm': gain(ks[4], (N_RET_LAYERS, D_MODEL)),
        'ret_w_in': nrm(ks[5], (N_RET_LAYERS, D_MODEL, ret_in), D_MODEL ** -0.5),
        'ret_decay_fwd': base_decay[None] + 0.05 * jax.random.normal(ks[6], (N_RET_LAYERS, RET_HEADS), f32),
        'ret_decay_bwd': base_decay[None] + 0.05 * jax.random.normal(ks[7], (N_RET_LAYERS, RET_HEADS), f32),
        'ret_gn': gain(ks[8], (N_RET_LAYERS, RET_HEADS, RET_DV)),
        'ret_w_out': nrm(ks[9], (N_RET_LAYERS, RET_HEADS * RET_DV, D_MODEL), (RET_HEADS * RET_DV) ** -0.5),
        'att_norm': gain(ks[10], (N_ATT_LAYERS, D_MODEL)),
        'att_w_in': nrm(ks[11], (N_ATT_LAYERS, D_MODEL, att_in), D_MODEL ** -0.5),
        'att_q_gain': gain(ks[12], (N_ATT_LAYERS, ATT_HEAD_DIM)),
        'att_k_gain': gain(ks[13], (N_ATT_LAYERS, ATT_HEAD_DIM)),
        'att_w_out': nrm(ks[14], (N_ATT_LAYERS, ATT_HEADS * ATT_HEAD_DIM, D_MODEL), (ATT_HEADS * ATT_HEAD_DIM) ** -0.5),
        'ffn_norm': gain(ks[15], (N_DENSE_LAYERS, D_MODEL)),
        'ffn_w_gate_up': nrm(ks[16], (N_DENSE_LAYERS, D_MODEL, 2 * D_FF), D_MODEL ** -0.5),
        'ffn_w_down': nrm(ks[17], (N_DENSE_LAYERS, D_FF, D_MODEL), D_FF ** -0.5),
        'moe_norm': gain(ks[18], (N_MOE_LAYERS, D_MODEL)),
        'moe_router': nrm(ks[19], (N_MOE_LAYERS, D_MODEL, N_EXPERTS), D_MODEL ** -0.5),
        'moe_w_gate_up': nrm(ks[20], (N_MOE_LAYERS, N_EXPERTS, D_MODEL, 2 * D_FF_EXPERT), D_MODEL ** -0.5),
        'moe_w_down': nrm(ks[21], (N_MOE_LAYERS, N_EXPERTS, D_FF_EXPERT, D_MODEL), D_FF_EXPERT ** -0.5),
        'ple_norm': gain(ks[22], (DEPTH, D_MODEL)),
        'ple_w_gate': nrm(ks[23], (DEPTH, D_MODEL, D_MODEL), D_MODEL ** -0.5),
        'ple_w_proj': nrm(ks[24], (DEPTH, PLE_DIM, D_MODEL), PLE_DIM ** -0.5),
    }


def reference(x_prompt, x_sample, p_prompt, p_sample,
              ret_norm, ret_w_in, ret_decay_fwd, ret_decay_bwd, ret_gn, ret_w_out,
              att_norm, att_w_in, att_q_gain, att_k_gain, att_w_out,
              ffn_norm, ffn_w_gate_up, ffn_w_down,
              moe_norm, moe_router, moe_w_gate_up, moe_w_down,
              ple_norm, ple_w_gate, ple_w_proj):
    w = {
        'ret_norm': ret_norm, 'ret_w_in': ret_w_in, 'ret_decay_fwd': ret_decay_fwd,
        'ret_decay_bwd': ret_decay_bwd, 'ret_gn': ret_gn, 'ret_w_out': ret_w_out,
        'att_norm': att_norm, 'att_w_in': att_w_in, 'att_q_gain': att_q_gain,
        'att_k_gain': att_k_gain, 'att_w_out': att_w_out,
        'ffn_norm': ffn_norm, 'ffn_w_gate_up': ffn_w_gate_up, 'ffn_w_down': ffn_w_down,
        'moe_norm': moe_norm, 'moe_router': moe_router, 'moe_w_gate_up': moe_w_gate_up,
        'moe_w_down': moe_w_down,
        'ple_norm': ple_norm, 'ple_w_gate': ple_w_gate, 'ple_w_proj': ple_w_proj,
    }
    y_prompt = run_trunk(x_prompt, p_prompt, w)
    y_sample = run_trunk(x_sample, p_sample, w)
    return (y_prompt, y_sample)
```

```python
import functools

import jax
import jax.numpy as jnp
from jax import lax
from jax.experimental import pallas as pl
from jax.experimental.pallas import tpu as pltpu

F32 = jnp.float32
BF16 = jnp.bfloat16

EPS = 1e-6
NEG_BIG = -1e30

RET_CHUNK = 256
ATT_HEADS = 16
ATT_HEAD_DIM = 64
ATT_GROUPS = ((128, 1), (512, 4), (2048, 16))
ATT_HALF = 64
ATT_TQ = 128
N_EXPERTS = 8
TOP_K = 2
LANES = 128
MXU_TILE = 256
VMEM_LIMIT = 56 * 1024 * 1024


def _cparams(sem, vmem=VMEM_LIMIT):
    return pltpu.CompilerParams(dimension_semantics=sem, vmem_limit_bytes=vmem)


def _rms_rows(x, g):
    ms = jnp.mean(x * x, axis=-1, keepdims=True)
    return x * lax.rsqrt(ms + EPS) * g


def _row_tile(t, pref):
    return pref if t % pref == 0 else t


def _ff_tile(ff, pref=512):
    for tf in range(pref, 0, -LANES):
        if ff % tf == 0:
            return tf
    return ff


def _norm_matmul_kernel(x_ref, g_ref, w_ref, o_ref, xs_ref):
    @pl.when(pl.program_id(1) == 0)
    def _():
        xs_ref[...] = _rms_rows(x_ref[...], g_ref[...]).astype(BF16)

    o_ref[...] = jnp.dot(xs_ref[...], w_ref[...], preferred_element_type=F32).astype(o_ref.dtype)


def norm_matmul(x, g, w, *, tm=1024, tn=1024, out_dtype=BF16):
    t, d = x.shape
    n = w.shape[1]
    tm = _row_tile(t, tm)
    return pl.pallas_call(
        _norm_matmul_kernel,
        out_shape=jax.ShapeDtypeStruct((t, n), out_dtype),
        grid=(t // tm, n // tn),
        in_specs=[
            pl.BlockSpec((tm, d), lambda i, j: (i, 0)),
            pl.BlockSpec((1, d), lambda i, j: (0, 0)),
            pl.BlockSpec((d, tn), lambda i, j: (0, j)),
        ],
        out_specs=pl.BlockSpec((tm, tn), lambda i, j: (i, j)),
        scratch_shapes=[pltpu.VMEM((tm, d), BF16)],
        compiler_params=_cparams(("parallel", "arbitrary")),
        name="norm_matmul",
    )(x, g.reshape(1, d), w)


def _att_inproj_kernel(x_ref, g_ref, w_ref, gain_ref, bd_ref, o_ref, xs_ref, *, n_qk_blocks):
    j = pl.program_id(1)

    @pl.when(j == 0)
    def _():
        xs_ref[...] = _rms_rows(x_ref[...], g_ref[...]).astype(BF16)

    y = jnp.dot(xs_ref[...], w_ref[...], preferred_element_type=F32)
    tn = y.shape[1]

    @pl.when(j < n_qk_blocks)
    def _():
        for c in range(tn // MXU_TILE):
            cs = slice(c * MXU_TILE, (c + 1) * MXU_TILE)
            yc = y[:, cs]
            ss = jnp.dot((yc * yc).astype(BF16), bd_ref[...], preferred_element_type=F32)
            yn = yc * lax.rsqrt(ss * (1.0 / ATT_HEAD_DIM) + EPS) * gain_ref[:, cs]
            o_ref[:, cs] = yn.astype(o_ref.dtype)

    @pl.when(j >= n_qk_blocks)
    def _():
        o_ref[...] = y.astype(o_ref.dtype)


def att_inproj(x, g, w, q_gain, k_gain, *, tm=1024):
    t, d = x.shape
    n = w.shape[1]
    tn = ATT_HEADS * ATT_HEAD_DIM
    n_groups = len(ATT_GROUPS)
    tm = _row_tile(t, tm)
    gains = jnp.stack([jnp.tile(q_gain, ATT_HEADS) * (ATT_HEAD_DIM ** -0.5),
                       jnp.tile(k_gain, ATT_HEADS)]).reshape(2, 1, tn).astype(F32)
    r = jnp.arange(MXU_TILE) // ATT_HEAD_DIM
    bd = (r[:, None] == r[None, :]).astype(BF16)
    return pl.pallas_call(
        functools.partial(_att_inproj_kernel, n_qk_blocks=2 * n_groups),
        out_shape=jax.ShapeDtypeStruct((t, n), BF16),
        grid=(t // tm, n // tn),
        in_specs=[
            pl.BlockSpec((tm, d), lambda i, j: (i, 0)),
            pl.BlockSpec((1, d), lambda i, j: (0, 0)),
            pl.BlockSpec((d, tn), lambda i, j: (0, j)),
            pl.BlockSpec((None, 1, tn), lambda i, j: (jnp.minimum(j // n_groups, 1), 0, 0)),
            pl.BlockSpec((MXU_TILE, MXU_TILE), lambda i, j: (0, 0)),
        ],
        out_specs=pl.BlockSpec((tm, tn), lambda i, j: (i, j)),
        scratch_shapes=[pltpu.VMEM((tm, d), BF16)],
        compiler_params=_cparams(("parallel", "arbitrary")),
        name="att_inproj",
    )(x, g.reshape(1, d), w, gains, bd)


def _retention_kernel(dec_ref, q_ref, k_ref, v_ref, g_ref, gn_ref, o_ref,
                      state_ref, bc_ref, dmat_ref, vec_ref, *, nc, C, H, dk, dv):
    s = pl.program_id(1)
    qscale = dk ** -0.5
    tn_dn = (((0,), (0,)), ((), ()))
    nt_dn = (((1,), (1,)), ((), ()))

    @pl.when(s == 0)
    def _tables():
        row = lax.broadcasted_iota(jnp.int32, (C, C), 0)
        col = lax.broadcasted_iota(jnp.int32, (C, C), 1)
        diff = (row - col).astype(F32)
        ridx = lax.broadcasted_iota(jnp.int32, (C, LANES), 0).astype(F32)
        for h in range(H):
            lf = dec_ref[0, h]
            lb = dec_ref[1, h]
            dmat_ref[h] = jnp.exp(jnp.where(diff >= 0, lf, -lb) * diff) * qscale
            vec_ref[0, h] = jnp.exp(lf * (ridx + 1.0)) * qscale
            vec_ref[1, h] = jnp.exp(lf * (C - 1.0 - ridx))
            vec_ref[2, h] = jnp.exp(lb * (C - ridx)) * qscale
            vec_ref[3, h] = jnp.exp(lb * ridx)

    @pl.when((s == 0) | (s == nc))
    def _zero():
        state_ref[...] = jnp.zeros_like(state_ref)

    def head_slices(h):
        return slice(h * dk, (h + 1) * dk), slice(h * dv, (h + 1) * dv)

    def state_step(h, kh, vh, st, zeta, chunk_decay):
        kz = (kh.astype(F32) * jnp.tile(zeta, (1, dk // LANES))).astype(BF16)
        upd = lax.dot_general(kz, vh, tn_dn, preferred_element_type=F32)
        state_ref[h] = st * chunk_decay + upd

    @pl.when(s < nc)
    def _backward():
        c = nc - 1 - s
        for h in range(H):
            ks, vs = head_slices(h)
            qh, kh, vh = q_ref[:, ks], k_ref[:, ks], v_ref[:, vs]
            st = state_ref[h]
            cross = jnp.dot(qh, st.astype(BF16), preferred_element_type=F32)
            bc_ref[c, :, vs] = cross * jnp.tile(vec_ref[2, h], (1, dv // LANES))
            state_step(h, kh, vh, st, vec_ref[3, h], dec_ref[3, h])

    @pl.when(s >= nc)
    def _forward():
        c = s - nc
        for h in range(H):
            ks, vs = head_slices(h)
            qh, kh, vh = q_ref[:, ks], k_ref[:, ks], v_ref[:, vs]
            st = state_ref[h]
            sr = lax.dot_general(qh, kh, nt_dn, preferred_element_type=F32)
            intra = jnp.dot((sr * dmat_ref[h]).astype(BF16), vh, preferred_element_type=F32)
            cross = jnp.dot(qh, st.astype(BF16), preferred_element_type=F32)
            o = intra + cross * jnp.tile(vec_ref[0, h], (1, dv // LANES)) + bc_ref[c, :, vs]
            state_step(h, kh, vh, st, vec_ref[1, h], dec_ref[2, h])
            on = _rms_rows(o, gn_ref[:, vs])
            gg = g_ref[:, vs].astype(F32)
            o_ref[:, vs] = (on * (gg * jax.nn.sigmoid(gg))).astype(o_ref.dtype)


def retention_core(proj, decay_fwd, decay_bwd, gn, *, batch, seq):
    H = decay_fwd.shape[0]
    dv = gn.shape[1]
    dk = dv // 2
    C = RET_CHUNK
    nc = seq // C
    qk_w, v_w = H * dk, H * dv
    proj3 = proj.reshape(batch, seq, 2 * qk_w + 2 * v_w)
    lg_f = jnp.log1p(-jnp.exp(decay_fwd.astype(F32)))
    lg_b = jnp.log1p(-jnp.exp(decay_bwd.astype(F32)))
    dec = jnp.stack([lg_f, lg_b, jnp.exp(lg_f * C), jnp.exp(lg_b * C)])

    def cidx(s):
        return jnp.where(s < nc, nc - 1 - s, s - nc)

    def oidx(s):
        return jnp.maximum(s - nc, 0)

    grid_spec = pltpu.PrefetchScalarGridSpec(
        num_scalar_prefetch=1,
        grid=(batch, 2 * nc),
        in_specs=[
            pl.BlockSpec((None, C, qk_w), lambda b, s, d: (b, cidx(s), 0)),
            pl.BlockSpec((None, C, qk_w), lambda b, s, d: (b, cidx(s), 1)),
            pl.BlockSpec((None, C, v_w), lambda b, s, d: (b, cidx(s), 2 * qk_w // v_w)),
            pl.BlockSpec((None, C, v_w), lambda b, s, d: (b, oidx(s), (2 * qk_w + v_w) // v_w)),
            pl.BlockSpec((1, v_w), lambda b, s, d: (0, 0)),
        ],
        out_specs=pl.BlockSpec((None, C, v_w), lambda b, s, d: (b, oidx(s), 0)),
        scratch_shapes=[
            pltpu.VMEM((H, dk, dv), F32),
            pltpu.VMEM((nc, C, v_w), F32),
            pltpu.VMEM((H, C, C), F32),
            pltpu.VMEM((4, H, C, LANES), F32),
        ],
    )
    out = pl.pallas_call(
        functools.partial(_retention_kernel, nc=nc, C=C, H=H, dk=dk, dv=dv),
        out_shape=jax.ShapeDtypeStruct((batch, seq, v_w), BF16),
        grid_spec=grid_spec,
        compiler_params=_cparams(("arbitrary", "arbitrary")),
        name="retention",
    )(dec, proj3, proj3, proj3, proj3, gn.reshape(1, v_w).astype(F32))
    return out.reshape(batch * seq, v_w)


def _matmul_res_kernel(a_ref, w_ref, r_ref, o_ref):
    o_ref[...] = r_ref[...] + jnp.dot(a_ref[...], w_ref[...], preferred_element_type=F32)


def matmul_residual(a, w, res, *, tm=1024):
    t, k = a.shape
    n = w.shape[1]
    tm = _row_tile(t, tm)
    return pl.pallas_call(
        _matmul_res_kernel,
        out_shape=jax.ShapeDtypeStruct((t, n), F32),
        grid=(t // tm,),
        in_specs=[
            pl.BlockSpec((tm, k), lambda i: (i, 0)),
            pl.BlockSpec((k, n), lambda i: (0, 0)),
            pl.BlockSpec((tm, n), lambda i: (i, 0)),
        ],
        out_specs=pl.BlockSpec((tm, n), lambda i: (i, 0)),
        compiler_params=_cparams(("parallel",)),
        name="matmul_residual",
    )(a, w, res)


def _ffn_kernel(eid_ref, nused_ref, x_ref, g_ref, wg_ref, wu_ref, wd_ref, o_ref, xs_ref, acc_ref,
                *, n_f, add_residual):
    i = pl.program_id(0)
    f = pl.program_id(1)
    used = i < nused_ref[0]

    @pl.when(used & (f == 0))
    def _():
        xs_ref[...] = _rms_rows(x_ref[...], g_ref[...]).astype(BF16)
        acc_ref[...] = jnp.zeros_like(acc_ref)

    @pl.when(used)
    def _():
        xs = xs_ref[...]
        a = jnp.dot(xs, wg_ref[...], preferred_element_type=F32)
        u = jnp.dot(xs, wu_ref[...], preferred_element_type=F32)
        hid = (a * jax.nn.sigmoid(a) * u).astype(BF16)
        acc_ref[...] += jnp.dot(hid, wd_ref[...], preferred_element_type=F32)

    @pl.when(used & (f == n_f - 1))
    def _():
        if add_residual:
            o_ref[...] = x_ref[...] + acc_ref[...]
        else:
            o_ref[...] = acc_ref[...]

    @pl.when(jnp.logical_not(used) & (f == n_f - 1))
    def _():
        o_ref[...] = jnp.zeros_like(o_ref)


def swiglu_ffn(x, g, w_gate_up, w_down, tile_expert, n_used, *, tm, tf, add_residual):
    t, d = x.shape
    ff = w_down.shape[1]
    n_f = ff // tf

    def fe(i, f, nu):
        return jnp.where(i < nu[0], f, 0)

    grid_spec = pltpu.PrefetchScalarGridSpec(
        num_scalar_prefetch=2,
        grid=(t // tm, n_f),
        in_specs=[
            pl.BlockSpec((tm, d), lambda i, f, e, nu: (i, 0)),
            pl.BlockSpec((1, d), lambda i, f, e, nu: (0, 0)),
            pl.BlockSpec((None, d, tf), lambda i, f, e, nu: (e[i], 0, fe(i, f, nu))),
            pl.BlockSpec((None, d, tf), lambda i, f, e, nu: (e[i], 0, n_f + fe(i, f, nu))),
            pl.BlockSpec((None, tf, d), lambda i, f, e, nu: (e[i], fe(i, f, nu), 0)),
        ],
        out_specs=pl.BlockSpec((tm, d), lambda i, f, e, nu: (i, 0)),
        scratch_shapes=[pltpu.VMEM((tm, d), BF16), pltpu.VMEM((tm, d), F32)],
    )
    return pl.pallas_call(
        functools.partial(_ffn_kernel, n_f=n_f, add_residual=add_residual),
        out_shape=jax.ShapeDtypeStruct((t, d), F32),
        grid_spec=grid_spec,
        compiler_params=_cparams(("parallel", "arbitrary")),
        name="swiglu_ffn",
    )(tile_expert, n_used, x, g.reshape(1, d), w_gate_up, w_gate_up, w_down)


def _ple_kernel(h_ref, p_ref, g_ref, wg_ref, wp_ref, o_ref):
    h = h_ref[...]
    hn = _rms_rows(h, g_ref[...]).astype(BF16)
    gate = jax.nn.sigmoid(jnp.dot(hn, wg_ref[...], preferred_element_type=F32))
    emb = jnp.dot(p_ref[...].astype(BF16), wp_ref[...], preferred_element_type=F32)
    o_ref[...] = h + gate * emb


def per_layer_embedding(h, p, g, w_gate, w_proj, *, tm=1024):
    t, d = h.shape
    pd = p.shape[1]
    tm = _row_tile(t, tm)
    return pl.pallas_call(
        _ple_kernel,
        out_shape=jax.ShapeDtypeStruct((t, d), F32),
        grid=(t // tm,),
        in_specs=[
            pl.BlockSpec((tm, d), lambda i: (i, 0)),
            pl.BlockSpec((tm, pd), lambda i: (i, 0)),
            pl.BlockSpec((1, d), lambda i: (0, 0)),
            pl.BlockSpec((d, d), lambda i: (0, 0)),
            pl.BlockSpec((pd, d), lambda i: (0, 0)),
        ],
        out_specs=pl.BlockSpec((tm, d), lambda i: (i, 0)),
        compiler_params=_cparams(("parallel",)),
        name="per_layer_embedding",
    )(h, p, g.reshape(1, d), w_gate, w_proj)


def _alibi_slope(h):
    return 2.0 ** (-8.0 * (h + 1) / ATT_HEADS)


def _attn_kernel(q_ref, k_ref, v_ref, o_ref, lse_ref, *, L, Tq, Tk, dil):
    i = pl.program_id(1)
    t0 = i * Tq
    start = pl.multiple_of(jnp.clip(t0 - ATT_HALF, 0, L - Tk), ATT_HALF)
    ri = lax.broadcasted_iota(jnp.int32, (Tq, Tk), 0)
    ci = lax.broadcasted_iota(jnp.int32, (Tq, Tk), 1)
    absrel = jnp.abs(ci - ri + (start - t0))
    valid = absrel <= ATT_HALF
    dist = absrel.astype(F32) * float(dil)
    lo = lax.broadcasted_iota(jnp.int32, (1, LANES), 1) < ATT_HEAD_DIM
    lane_t = lax.broadcasted_iota(jnp.int32, (Tq, LANES), 1)
    lse_tile = jnp.zeros((Tq, LANES), F32)
    nt_dn = (((1,), (1,)), ((), ()))

    for p in range(ATT_HEADS // 2):
        cs = slice(LANES * p, LANES * (p + 1))
        q2 = q_ref[:, cs]
        k2 = k_ref[pl.ds(start, Tk), cs]
        v2 = v_ref[pl.ds(start, Tk), cs]
        per_head = []
        for hh in range(2):
            sel = lo if hh == 0 else jnp.logical_not(lo)
            qh = jnp.where(sel, q2, jnp.zeros_like(q2))
            s = lax.dot_general(qh, k2, nt_dn, preferred_element_type=F32)
            s = jnp.where(valid, s - dist * _alibi_slope(2 * p + hh), NEG_BIG)
            m = jnp.max(s, axis=-1, keepdims=True)
            pr = jnp.exp(s - m).astype(BF16)
            vext = jnp.where(sel, v2, jnp.ones_like(v2))
            per_head.append((jnp.dot(pr, vext, preferred_element_type=F32), m))
        (of0, m0), (of1, m1) = per_head
        num = jnp.where(lo, of0, of1)
        den = pltpu.roll(jnp.where(lo, of1, of0), ATT_HEAD_DIM, axis=1)
        o_ref[:, cs] = (num / den).astype(o_ref.dtype)
        lse0 = m0 + jnp.log(of0[:, ATT_HEAD_DIM:ATT_HEAD_DIM + 1])
        lse1 = m1 + jnp.log(of1[:, 0:1])
        lse_tile = jnp.where(lane_t == 2 * p, lse0, jnp.where(lane_t == 2 * p + 1, lse1, lse_tile))
    lse_ref[...] = lse_tile


def window_attention(q, k, v, cols, *, dil):
    bp, L = q.shape[0], q.shape[1]
    hd = ATT_HEADS * ATT_HEAD_DIM
    Tq = min(ATT_TQ, L)
    Tk = min(L, Tq + 2 * ATT_HALF)
    qc, kc, vc = cols
    return pl.pallas_call(
        functools.partial(_attn_kernel, L=L, Tq=Tq, Tk=Tk, dil=dil),
        out_shape=(jax.ShapeDtypeStruct((bp, L, hd), BF16),
                   jax.ShapeDtypeStruct((bp, L, LANES), F32)),
        grid=(bp, L // Tq),
        in_specs=[
            pl.BlockSpec((None, Tq, hd), lambda b, i: (b, i, qc)),
            pl.BlockSpec((None, L, hd), lambda b, i: (b, 0, kc)),
            pl.BlockSpec((None, L, hd), lambda b, i: (b, 0, vc)),
        ],
        out_specs=(pl.BlockSpec((None, Tq, hd), lambda b, i: (b, i, 0)),
                   pl.BlockSpec((None, Tq, LANES), lambda b, i: (b, i, 0))),
        compiler_params=_cparams(("parallel", "arbitrary")),
        name=f"window_attention_d{dil}",
    )(q, k, v)


def _merge_kernel(o0_ref, o1_ref, o2_ref, l0_ref, l1_ref, l2_ref, e_ref, w_ref, r_ref, out_ref):
    ls = [l0_ref[...], l1_ref[...], l2_ref[...]]
    m = jnp.maximum(jnp.maximum(ls[0], ls[1]), ls[2])
    es = [jnp.exp(l - m) for l in ls]
    inv = 1.0 / (es[0] + es[1] + es[2])
    acc = None
    for e, o_ref in zip(es, (o0_ref, o1_ref, o2_ref)):
        wt = e * inv
        hi = wt.astype(BF16)
        lo = (wt - hi.astype(F32)).astype(BF16)
        wfull = (jnp.dot(hi, e_ref[...], preferred_element_type=F32)
                 + jnp.dot(lo, e_ref[...], preferred_element_type=F32))
        term = wfull * o_ref[...].astype(F32)
        acc = term if acc is None else acc + term
    out_ref[...] = r_ref[...] + jnp.dot(acc.astype(BF16), w_ref[...], preferred_element_type=F32)


def merge_groups_project(outs, lses, w_out, res, *, tm=1024):
    t, hd = outs[0].shape
    n = w_out.shape[1]
    tm = _row_tile(t, tm)
    hrow = jnp.arange(LANES)[:, None]
    hcol = jnp.arange(hd)[None, :] // ATT_HEAD_DIM
    expand = (hrow == hcol).astype(BF16)
    o_spec = pl.BlockSpec((tm, hd), lambda i: (i, 0))
    l_spec = pl.BlockSpec((tm, LANES), lambda i: (i, 0))
    return pl.pallas_call(
        _merge_kernel,
        out_shape=jax.ShapeDtypeStruct((t, n), F32),
        grid=(t // tm,),
        in_specs=[o_spec, o_spec, o_spec, l_spec, l_spec, l_spec,
                  pl.BlockSpec((LANES, hd), lambda i: (0, 0)),
                  pl.BlockSpec((hd, n), lambda i: (0, 0)),
                  pl.BlockSpec((tm, n), lambda i: (i, 0))],
        out_specs=pl.BlockSpec((tm, n), lambda i: (i, 0)),
        compiler_params=_cparams(("parallel",)),
        name="merge_groups_project",
    )(*outs, *lses, expand, w_out, res)


def _router_kernel(x_ref, g_ref, w_ref, idx_ref, wt_ref):
    hn = _rms_rows(x_ref[...], g_ref[...]).astype(BF16)
    logits = jnp.dot(hn, w_ref[...], preferred_element_type=F32)
    lane = lax.broadcasted_iota(jnp.int32, logits.shape, 1)
    neg_inf = jnp.float32(-jnp.inf)
    l1 = jnp.where(lane < N_EXPERTS, logits, neg_inf)
    m1 = jnp.max(l1, axis=-1, keepdims=True)
    lane_f = lane.astype(F32)
    i1 = jnp.min(jnp.where(l1 == m1, lane_f, float(LANES)), axis=-1, keepdims=True).astype(jnp.int32)
    l2 = jnp.where(lane == i1, neg_inf, l1)
    m2 = jnp.max(l2, axis=-1, keepdims=True)
    i2 = jnp.min(jnp.where(l2 == m2, lane_f, float(LANES)), axis=-1, keepdims=True).astype(jnp.int32)
    e2 = jnp.exp(m2 - m1)
    denom = 1.0 + e2
    idx_ref[...] = jnp.where(lane == 0, i1, jnp.where(lane == 1, i2, 0))
    wt_ref[...] = jnp.where(lane == 0, 1.0 / denom, jnp.where(lane == 1, e2 / denom, 0.0))


def moe_router(x, g, router_w, *, tm=1024):
    t, d = x.shape
    tm = _row_tile(t, tm)
    w = jnp.zeros((d, LANES), BF16).at[:, :N_EXPERTS].set(router_w.astype(BF16))
    return pl.pallas_call(
        _router_kernel,
        out_shape=(jax.ShapeDtypeStruct((t, LANES), jnp.int32),
                   jax.ShapeDtypeStruct((t, LANES), F32)),
        grid=(t // tm,),
        in_specs=[pl.BlockSpec((tm, d), lambda i: (i, 0)),
                  pl.BlockSpec((1, d), lambda i: (0, 0)),
                  pl.BlockSpec((d, LANES), lambda i: (0, 0))],
        out_specs=(pl.BlockSpec((tm, LANES), lambda i: (i, 0)),
                   pl.BlockSpec((tm, LANES), lambda i: (i, 0))),
        compiler_params=_cparams(("parallel",)),
        name="moe_router",
    )(x, g.reshape(1, d), w)


def _row_copy(src, src_row, dst, dst_row, sem):
    return pltpu.make_async_copy(src.at[pl.ds(src_row, 1)], dst.at[pl.ds(dst_row, 1)], sem)


def _scatter_kernel(pos_ref, x_ref, init_ref, xs_hbm, sem, *, tm):
    del init_ref

    def issue(r, carry):
        for k in range(TOP_K):
            _row_copy(x_ref, r, xs_hbm, pos_ref[TOP_K * r + k], sem).start()
        return carry

    lax.fori_loop(0, tm, issue, 0)

    def drain(r, carry):
        for k in range(TOP_K):
            _row_copy(x_ref, 0, xs_hbm, 0, sem).wait()
        return carry

    lax.fori_loop(0, tm, drain, 0)


def moe_scatter_rows(x, pos_flat, n_slots, *, tm=512):
    t, d = x.shape
    tm = _row_tile(t, tm)
    init = jnp.zeros((n_slots, d), x.dtype)
    return pl.pallas_call(
        functools.partial(_scatter_kernel, tm=tm),
        out_shape=jax.ShapeDtypeStruct((n_slots, d), x.dtype),
        grid=(t // tm,),
        in_specs=[pl.BlockSpec((TOP_K * tm,), lambda i: (i,), memory_space=pltpu.SMEM),
                  pl.BlockSpec((tm, d), lambda i: (i, 0)),
                  pl.BlockSpec(memory_space=pl.ANY)],
        out_specs=pl.BlockSpec(memory_space=pl.ANY),
        scratch_shapes=[pltpu.SemaphoreType.DMA(())],
        input_output_aliases={2: 0},
        compiler_params=_cparams(("arbitrary",)),
        name="moe_scatter_rows",
    )(pos_flat, x, init)


def _combine_kernel(pos_ref, h_ref, wt_ref, y_hbm, o_ref, ya_ref, yb_ref, sem, *, tm):
    bufs = (ya_ref, yb_ref)

    def issue(r, carry):
        for k in range(TOP_K):
            _row_copy(y_hbm, pos_ref[TOP_K * r + k], bufs[k], r, sem).start()
        return carry

    lax.fori_loop(0, tm, issue, 0)

    def drain(r, carry):
        for k in range(TOP_K):
            _row_copy(y_hbm, 0, bufs[k], 0, sem).wait()
        return carry

    lax.fori_loop(0, tm, drain, 0)
    wt = wt_ref[...]
    o_ref[...] = h_ref[...] + wt[:, 0:1] * ya_ref[...] + wt[:, 1:2] * yb_ref[...]


def moe_combine(h, wt, y_sorted, pos_flat, *, tm=512):
    t, d = h.shape
    tm = _row_tile(t, tm)
    return pl.pallas_call(
        functools.partial(_combine_kernel, tm=tm),
        out_shape=jax.ShapeDtypeStruct((t, d), F32),
        grid=(t // tm,),
        in_specs=[pl.BlockSpec((TOP_K * tm,), lambda i: (i,), memory_space=pltpu.SMEM),
                  pl.BlockSpec((tm, d), lambda i: (i, 0)),
                  pl.BlockSpec((tm, LANES), lambda i: (i, 0)),
                  pl.BlockSpec(memory_space=pl.ANY)],
        out_specs=pl.BlockSpec((tm, d), lambda i: (i, 0)),
        scratch_shapes=[pltpu.VMEM((tm, d), F32), pltpu.VMEM((tm, d), F32),
                        pltpu.SemaphoreType.DMA(())],
        compiler_params=_cparams(("arbitrary",)),
        name="moe_combine",
    )(pos_flat, h, wt, y_sorted)


def _moe_tile_rows(t):
    return 1024 if t >= 16384 else 256


def moe_ffn(h, norm_g, router_w, w_gate_up, w_down):
    t, d = h.shape
    tm = _moe_tile_rows(t)
    idx, wt = moe_router(h, norm_g, router_w)
    e_flat = idx[:, :TOP_K].reshape(-1)
    onehot = (e_flat[:, None] == jnp.arange(N_EXPERTS)[None, :]).astype(jnp.int32)
    csum = jnp.cumsum(onehot, axis=0)
    rank = jnp.take_along_axis(csum, e_flat[:, None], axis=1)[:, 0] - 1
    counts = csum[-1]
    tiles_per = (counts + tm - 1) // tm
    tile_end = jnp.cumsum(tiles_per)
    starts = (tile_end - tiles_per) * tm
    pos = (starts[e_flat] + rank).astype(jnp.int32)
    n_tiles = (TOP_K * t) // tm + N_EXPERTS
    tile_ids = jnp.arange(n_tiles)
    tile_expert = jnp.minimum(jnp.sum(tile_ids[:, None] >= tile_end[None, :], axis=1),
                              N_EXPERTS - 1).astype(jnp.int32)
    n_used = tile_end[-1:].astype(jnp.int32)

    xs = moe_scatter_rows(h, pos, n_tiles * tm)
    ys = swiglu_ffn(xs, norm_g, w_gate_up, w_down, tile_expert, n_used, tm=tm,
                    tf=_ff_tile(w_down.shape[1]), add_residual=False)
    return moe_combine(h, wt, ys, pos)


def retention_mixer(h, norm_g, w_in, decay_fwd, decay_bwd, gn, w_out, *, batch, seq):
    proj = norm_matmul(h, norm_g, w_in)
    gated = retention_core(proj, decay_fwd, decay_bwd, gn, batch=batch, seq=seq)
    return matmul_residual(gated, w_out, h)


def _to_residue_major(a, batch, seq, dil):
    L = seq // dil
    return a.reshape(batch, L, dil, a.shape[-1]).transpose(0, 2, 1, 3).reshape(batch * dil, L, a.shape[-1])


def _from_residue_major(a, batch, seq, dil):
    L = seq // dil
    return a.reshape(batch, dil, L, a.shape[-1]).transpose(0, 2, 1, 3).reshape(batch * seq, a.shape[-1])


def dilated_attention_mixer(h, norm_g, w_in, q_gain, k_gain, w_out, *, batch, seq):
    hd = ATT_HEADS * ATT_HEAD_DIM
    n_groups = len(ATT_GROUPS)
    proj = att_inproj(h, norm_g, w_in, q_gain, k_gain).reshape(batch, seq, 3 * n_groups * hd)
    outs, lses = [], []
    for g, (_, dil) in enumerate(ATT_GROUPS):
        if dil == 1:
            o, lse = window_attention(proj, proj, proj, (g, n_groups + g, 2 * n_groups + g), dil=1)
        else:
            def pick(c):
                blk = proj[:, :, (c * n_groups + g) * hd:(c * n_groups + g + 1) * hd]
                return _to_residue_major(blk, batch, seq, dil)
            o, lse = window_attention(pick(0), pick(1), pick(2), (0, 0, 0), dil=dil)
        outs.append(_from_residue_major(o, batch, seq, dil))
        lses.append(_from_residue_major(lse, batch, seq, dil))
    return merge_groups_project(outs, lses, w_out, h)


def dense_ffn(h, norm_g, w_gate_up, w_down):
    t = h.shape[0]
    ff = w_down.shape[0]
    zero = jnp.zeros((t // _row_tile(t, 1024),), jnp.int32)
    n_used = jnp.full((1,), zero.shape[0], jnp.int32)
    return swiglu_ffn(h, norm_g, w_gate_up[None], w_down[None], zero, n_used,
                      tm=_row_tile(t, 1024), tf=_ff_tile(ff), add_residual=True)


def run_trunk(x, p, w):
    batch, seq, d = x.shape
    depth = p.shape[0]
    h = x.reshape(batch * seq, d)
    p2 = p.reshape(depth, batch * seq, p.shape[-1])
    for i in range(depth):
        j = i // 2
        if i % 2 == 0:
            h = retention_mixer(h, w['ret_norm'][j], w['ret_w_in'][j], w['ret_decay_fwd'][j],
                                w['ret_decay_bwd'][j], w['ret_gn'][j], w['ret_w_out'][j],
                                batch=batch, seq=seq)
            h = dense_ffn(h, w['ffn_norm'][j], w['ffn_w_gate_up'][j], w['ffn_w_down'][j])
        else:
            h = dilated_attention_mixer(h, w['att_norm'][j], w['att_w_in'][j], w['att_q_gain'][j],
                                        w['att_k_gain'][j], w['att_w_out'][j], batch=batch, seq=seq)
            h = moe_ffn(h, w['moe_norm'][j], w['moe_router'][j], w['moe_w_gate_up'][j],
                        w['moe_w_down'][j])
        h = per_layer_embedding(h, p2[i], w['ple_norm'][i], w['ple_w_gate'][i], w['ple_w_proj'][i])
    return h.reshape(batch, seq, d)


_MATMUL_WEIGHTS = ('ret_w_in', 'ret_w_out', 'att_w_in', 'att_w_out', 'ffn_w_gate_up', 'ffn_w_down',
                   'moe_w_gate_up', 'moe_w_down', 'ple_w_gate', 'ple_w_proj')


def kernel(x_prompt, x_sample, p_prompt, p_sample,
           ret_norm, ret_w_in, ret_decay_fwd, ret_decay_bwd, ret_gn, ret_w_out,
           att_norm, att_w_in, att_q_gain, att_k_gain, att_w_out,
           ffn_norm, ffn_w_gate_up, ffn_w_down,
           moe_norm, moe_router, moe_w_gate_up, moe_w_down,
           ple_norm, ple_w_gate, ple_w_proj):
    w = {
        'ret_norm': ret_norm, 'ret_w_in': ret_w_in, 'ret_decay_fwd': ret_decay_fwd,
        'ret_decay_bwd': ret_decay_bwd, 'ret_gn': ret_gn, 'ret_w_out': ret_w_out,
        'att_norm': att_norm, 'att_w_in': att_w_in, 'att_q_gain': att_q_gain,
        'att_k_gain': att_k_gain, 'att_w_out': att_w_out,
        'ffn_norm': ffn_norm, 'ffn_w_gate_up': ffn_w_gate_up, 'ffn_w_down': ffn_w_down,
        'moe_norm': moe_norm, 'moe_router': moe_router, 'moe_w_gate_up': moe_w_gate_up,
        'moe_w_down': moe_w_down,
        'ple_norm': ple_norm, 'ple_w_gate': ple_w_gate, 'ple_w_proj': ple_w_proj,
    }
    for name in _MATMUL_WEIGHTS:
        w[name] = w[name].astype(BF16)
    nb = x_prompt.shape[0]
    x = jnp.concatenate([x_prompt, x_sample], axis=0)
    p = jnp.concatenate([p_prompt, p_sample], axis=1)
    y = run_trunk(x, p, w)
    return (y[:nb], y[nb:])
```

```python
import functools

import jax
import jax.numpy as jnp
from jax import lax
from jax.experimental import pallas as pl
from jax.experimental.pallas import tpu as pltpu

F32 = jnp.float32
BF16 = jnp.bfloat16

EPS = 1e-6
NEG_BIG = -1e30

RET_CHUNK = 256
ATT_HEADS = 16
ATT_HEAD_DIM = 64
ATT_GROUPS = ((128, 1), (512, 4), (2048, 16))
ATT_HALF = 64
ATT_TQ = 256
N_EXPERTS = 8
TOP_K = 2
LANES = 128
MXU_TILE = 256
VMEM_LIMIT = 56 * 1024 * 1024


def _cparams(sem, vmem=VMEM_LIMIT):
    return pltpu.CompilerParams(dimension_semantics=sem, vmem_limit_bytes=vmem)


def _rms_rows(x, g):
    ms = jnp.mean(x * x, axis=-1, keepdims=True)
    return x * lax.rsqrt(ms + EPS) * g


def _row_tile(t, pref):
    return pref if t % pref == 0 else t


def _ff_tile(ff, pref=512):
    for tf in range(pref, 0, -LANES):
        if ff % tf == 0:
            return tf
    return ff


def _norm_matmul_kernel(x_ref, g_ref, w_ref, o_ref, xs_ref):
    @pl.when(pl.program_id(1) == 0)
    def _():
        xs_ref[...] = _rms_rows(x_ref[...], g_ref[...]).astype(BF16)

    o_ref[...] = jnp.dot(xs_ref[...], w_ref[...], preferred_element_type=F32).astype(o_ref.dtype)


def norm_matmul(x, g, w, *, tm=1024, tn=1024, out_dtype=BF16):
    t, d = x.shape
    n = w.shape[1]
    tm = _row_tile(t, tm)
    return pl.pallas_call(
        _norm_matmul_kernel,
        out_shape=jax.ShapeDtypeStruct((t, n), out_dtype),
        grid=(t // tm, n // tn),
        in_specs=[
            pl.BlockSpec((tm, d), lambda i, j: (i, 0)),
            pl.BlockSpec((1, d), lambda i, j: (0, 0)),
            pl.BlockSpec((d, tn), lambda i, j: (0, j)),
        ],
        out_specs=pl.BlockSpec((tm, tn), lambda i, j: (i, j)),
        scratch_shapes=[pltpu.VMEM((tm, d), BF16)],
        compiler_params=_cparams(("parallel", "arbitrary")),
        name="norm_matmul",
    )(x, g.reshape(1, d), w)


def _att_inproj_kernel(x_ref, g_ref, w_ref, gain_ref, bd_ref, o_ref, xs_ref, *, n_qk_blocks):
    j = pl.program_id(1)

    @pl.when(j == 0)
    def _():
        xs_ref[...] = _rms_rows(x_ref[...], g_ref[...]).astype(BF16)

    y = jnp.dot(xs_ref[...], w_ref[...], preferred_element_type=F32)
    tn = y.shape[1]

    @pl.when(j < n_qk_blocks)
    def _():
        for c in range(tn // MXU_TILE):
            cs = slice(c * MXU_TILE, (c + 1) * MXU_TILE)
            yc = y[:, cs]
            ss = jnp.dot((yc * yc).astype(BF16), bd_ref[...], preferred_element_type=F32)
            yn = yc * lax.rsqrt(ss * (1.0 / ATT_HEAD_DIM) + EPS) * gain_ref[:, cs]
            o_ref[:, cs] = yn.astype(o_ref.dtype)

    @pl.when(j >= n_qk_blocks)
    def _():
        o_ref[...] = y.astype(o_ref.dtype)


def att_inproj(x, g, w, q_gain, k_gain, *, tm=1024):
    t, d = x.shape
    n = w.shape[1]
    tn = ATT_HEADS * ATT_HEAD_DIM
    n_groups = len(ATT_GROUPS)
    tm = _row_tile(t, tm)
    gains = jnp.stack([jnp.tile(q_gain, ATT_HEADS) * (ATT_HEAD_DIM ** -0.5),
                       jnp.tile(k_gain, ATT_HEADS)]).reshape(2, 1, tn).astype(F32)
    r = jnp.arange(MXU_TILE) // ATT_HEAD_DIM
    bd = (r[:, None] == r[None, :]).astype(BF16)
    return pl.pallas_call(
        functools.partial(_att_inproj_kernel, n_qk_blocks=2 * n_groups),
        out_shape=jax.ShapeDtypeStruct((t, n), BF16),
        grid=(t // tm, n // tn),
        in_specs=[
            pl.BlockSpec((tm, d), lambda i, j: (i, 0)),
            pl.BlockSpec((1, d), lambda i, j: (0, 0)),
            pl.BlockSpec((d, tn), lambda i, j: (0, j)),
            pl.BlockSpec((None, 1, tn), lambda i, j: (jnp.minimum(j // n_groups, 1), 0, 0)),
            pl.BlockSpec((MXU_TILE, MXU_TILE), lambda i, j: (0, 0)),
        ],
        out_specs=pl.BlockSpec((tm, tn), lambda i, j: (i, j)),
        scratch_shapes=[pltpu.VMEM((tm, d), BF16)],
        compiler_params=_cparams(("parallel", "arbitrary")),
        name="att_inproj",
    )(x, g.reshape(1, d), w, gains, bd)


def _retention_kernel(dec_ref, q_ref, k_ref, v_ref, g_ref, gn_ref, o_ref,
                      state_ref, bc_ref, dmat_ref, vec_ref, *, nc, C, H, dk, dv):
    s = pl.program_id(1)
    qscale = dk ** -0.5
    tn_dn = (((0,), (0,)), ((), ()))
    nt_dn = (((1,), (1,)), ((), ()))

    @pl.when(s == 0)
    def _tables():
        row = lax.broadcasted_iota(jnp.int32, (C, C), 0)
        col = lax.broadcasted_iota(jnp.int32, (C, C), 1)
        diff = (row - col).astype(F32)
        ridx = lax.broadcasted_iota(jnp.int32, (C, LANES), 0).astype(F32)
        for h in range(H):
            lf = dec_ref[0, h]
            lb = dec_ref[1, h]
            dmat_ref[h] = jnp.exp(jnp.where(diff >= 0, lf, -lb) * diff) * qscale
            vec_ref[0, h] = jnp.exp(lf * (ridx + 1.0)) * qscale
            vec_ref[1, h] = jnp.exp(lf * (C - 1.0 - ridx))
            vec_ref[2, h] = jnp.exp(lb * (C - ridx)) * qscale
            vec_ref[3, h] = jnp.exp(lb * ridx)

    @pl.when((s == 0) | (s == nc))
    def _zero():
        state_ref[...] = jnp.zeros_like(state_ref)

    def head_slices(h):
        return slice(h * dk, (h + 1) * dk), slice(h * dv, (h + 1) * dv)

    def state_step(h, kh, vh, st, zeta, chunk_decay):
        kz = (kh.astype(F32) * jnp.tile(zeta, (1, dk // LANES))).astype(BF16)
        upd = lax.dot_general(kz, vh, tn_dn, preferred_element_type=F32)
        state_ref[h] = st * chunk_decay + upd

    @pl.when(s < nc)
    def _backward():
        c = nc - 1 - s
        for h in range(H):
            ks, vs = head_slices(h)
            qh, kh, vh = q_ref[:, ks], k_ref[:, ks], v_ref[:, vs]
            st = state_ref[h]
            cross = jnp.dot(qh, st.astype(BF16), preferred_element_type=F32)
            bc_ref[c, :, vs] = cross * jnp.tile(vec_ref[2, h], (1, dv // LANES))
            state_step(h, kh, vh, st, vec_ref[3, h], dec_ref[3, h])

    @pl.when(s >= nc)
    def _forward():
        c = s - nc
        for h in range(H):
            ks, vs = head_slices(h)
            qh, kh, vh = q_ref[:, ks], k_ref[:, ks], v_ref[:, vs]
            st = state_ref[h]
            sr = lax.dot_general(qh, kh, nt_dn, preferred_element_type=F32)
            intra = jnp.dot((sr * dmat_ref[h]).astype(BF16), vh, preferred_element_type=F32)
            cross = jnp.dot(qh, st.astype(BF16), preferred_element_type=F32)
            o = intra + cross * jnp.tile(vec_ref[0, h], (1, dv // LANES)) + bc_ref[c, :, vs]
            state_step(h, kh, vh, st, vec_ref[1, h], dec_ref[2, h])
            on = _rms_rows(o, gn_ref[:, vs])
            gg = g_ref[:, vs].astype(F32)
            o_ref[:, vs] = (on * (gg * jax.nn.sigmoid(gg))).astype(o_ref.dtype)


def retention_core(proj, decay_fwd, decay_bwd, gn, *, batch, seq):
    H = decay_fwd.shape[0]
    dv = gn.shape[1]
    dk = dv // 2
    C = RET_CHUNK
    nc = seq // C
    qk_w, v_w = H * dk, H * dv
    proj3 = proj.reshape(batch, seq, 2 * qk_w + 2 * v_w)
    lg_f = jnp.log1p(-jnp.exp(decay_fwd.astype(F32)))
    lg_b = jnp.log1p(-jnp.exp(decay_bwd.astype(F32)))
    dec = jnp.stack([lg_f, lg_b, jnp.exp(lg_f * C), jnp.exp(lg_b * C)])

    def cidx(s):
        return jnp.where(s < nc, nc - 1 - s, s - nc)

    def oidx(s):
        return jnp.maximum(s - nc, 0)

    grid_spec = pltpu.PrefetchScalarGridSpec(
        num_scalar_prefetch=1,
        grid=(batch, 2 * nc),
        in_specs=[
            pl.BlockSpec((None, C, qk_w), lambda b, s, d: (b, cidx(s), 0)),
            pl.BlockSpec((None, C, qk_w), lambda b, s, d: (b, cidx(s), 1)),
            pl.BlockSpec((None, C, v_w), lambda b, s, d: (b, cidx(s), 2 * qk_w // v_w)),
            pl.BlockSpec((None, C, v_w), lambda b, s, d: (b, oidx(s), (2 * qk_w + v_w) // v_w)),
            pl.BlockSpec((1, v_w), lambda b, s, d: (0, 0)),
        ],
        out_specs=pl.BlockSpec((None, C, v_w), lambda b, s, d: (b, oidx(s), 0)),
        scratch_shapes=[
            pltpu.VMEM((H, dk, dv), F32),
            pltpu.VMEM((nc, C, v_w), F32),
            pltpu.VMEM((H, C, C), F32),
            pltpu.VMEM((4, H, C, LANES), F32),
        ],
    )
    out = pl.pallas_call(
        functools.partial(_retention_kernel, nc=nc, C=C, H=H, dk=dk, dv=dv),
        out_shape=jax.ShapeDtypeStruct((batch, seq, v_w), BF16),
        grid_spec=grid_spec,
        compiler_params=_cparams(("arbitrary", "arbitrary")),
        name="retention",
    )(dec, proj3, proj3, proj3, proj3, gn.reshape(1, v_w).astype(F32))
    return out.reshape(batch * seq, v_w)


def _matmul_res_kernel(a_ref, w_ref, r_ref, o_ref):
    o_ref[...] = r_ref[...] + jnp.dot(a_ref[...], w_ref[...], preferred_element_type=F32)


def matmul_residual(a, w, res, *, tm=1024):
    t, k = a.shape
    n = w.shape[1]
    tm = _row_tile(t, tm)
    return pl.pallas_call(
        _matmul_res_kernel,
        out_shape=jax.ShapeDtypeStruct((t, n), F32),
        grid=(t // tm,),
        in_specs=[
            pl.BlockSpec((tm, k), lambda i: (i, 0)),
            pl.BlockSpec((k, n), lambda i: (0, 0)),
            pl.BlockSpec((tm, n), lambda i: (i, 0)),
        ],
        out_specs=pl.BlockSpec((tm, n), lambda i: (i, 0)),
        compiler_params=_cparams(("parallel",)),
        name="matmul_residual",
    )(a, w, res)


def _ffn_kernel(eid_ref, nused_ref, x_ref, g_ref, wg_ref, wu_ref, wd_ref, o_ref, xs_ref, acc_ref,
                *, n_f, add_residual):
    i = pl.program_id(0)
    f = pl.program_id(1)
    used = i < nused_ref[0]

    @pl.when(used & (f == 0))
    def _():
        xs_ref[...] = _rms_rows(x_ref[...], g_ref[...]).astype(BF16)
        acc_ref[...] = jnp.zeros_like(acc_ref)

    @pl.when(used)
    def _():
        xs = xs_ref[...]
        a = jnp.dot(xs, wg_ref[...], preferred_element_type=F32)
        u = jnp.dot(xs, wu_ref[...], preferred_element_type=F32)
        hid = (a * jax.nn.sigmoid(a) * u).astype(BF16)
        acc_ref[...] += jnp.dot(hid, wd_ref[...], preferred_element_type=F32)

    @pl.when(used & (f == n_f - 1))
    def _():
        if add_residual:
            o_ref[...] = x_ref[...] + acc_ref[...]
        else:
            o_ref[...] = acc_ref[...]

    @pl.when(jnp.logical_not(used) & (f == n_f - 1))
    def _():
        o_ref[...] = jnp.zeros_like(o_ref)


def swiglu_ffn(x, g, w_gate_up, w_down, tile_expert, n_used, *, tm, tf, add_residual):
    t, d = x.shape
    ff = w_down.shape[1]
    n_f = ff // tf

    def fe(i, f, nu):
        return jnp.where(i < nu[0], f, 0)

    grid_spec = pltpu.PrefetchScalarGridSpec(
        num_scalar_prefetch=2,
        grid=(t // tm, n_f),
        in_specs=[
            pl.BlockSpec((tm, d), lambda i, f, e, nu: (i, 0)),
            pl.BlockSpec((1, d), lambda i, f, e, nu: (0, 0)),
            pl.BlockSpec((None, d, tf), lambda i, f, e, nu: (e[i], 0, fe(i, f, nu))),
            pl.BlockSpec((None, d, tf), lambda i, f, e, nu: (e[i], 0, n_f + fe(i, f, nu))),
            pl.BlockSpec((None, tf, d), lambda i, f, e, nu: (e[i], fe(i, f, nu), 0)),
        ],
        out_specs=pl.BlockSpec((tm, d), lambda i, f, e, nu: (i, 0)),
        scratch_shapes=[pltpu.VMEM((tm, d), BF16), pltpu.VMEM((tm, d), F32)],
    )
    return pl.pallas_call(
        functools.partial(_ffn_kernel, n_f=n_f, add_residual=add_residual),
        out_shape=jax.ShapeDtypeStruct((t, d), F32),
        grid_spec=grid_spec,
        compiler_params=_cparams(("parallel", "arbitrary")),
        name="swiglu_ffn",
    )(tile_expert, n_used, x, g.reshape(1, d), w_gate_up, w_gate_up, w_down)


def _ple_kernel(h_ref, p_ref, g_ref, wg_ref, wp_ref, o_ref):
    h = h_ref[...]
    hn = _rms_rows(h, g_ref[...]).astype(BF16)
    gate = jax.nn.sigmoid(jnp.dot(hn, wg_ref[...], preferred_element_type=F32))
    emb = jnp.dot(p_ref[...].astype(BF16), wp_ref[...], preferred_element_type=F32)
    o_ref[...] = h + gate * emb


def per_layer_embedding(h, p, g, w_gate, w_proj, *, tm=1024):
    t, d = h.shape
    pd = p.shape[1]
    tm = _row_tile(t, tm)
    return pl.pallas_call(
        _ple_kernel,
        out_shape=jax.ShapeDtypeStruct((t, d), F32),
        grid=(t // tm,),
        in_specs=[
            pl.BlockSpec((tm, d), lambda i: (i, 0)),
            pl.BlockSpec((tm, pd), lambda i: (i, 0)),
            pl.BlockSpec((1, d), lambda i: (0, 0)),
            pl.BlockSpec((d, d), lambda i: (0, 0)),
            pl.BlockSpec((pd, d), lambda i: (0, 0)),
        ],
        out_specs=pl.BlockSpec((tm, d), lambda i: (i, 0)),
        compiler_params=_cparams(("parallel",)),
        name="per_layer_embedding",
    )(h, p, g.reshape(1, d), w_gate, w_proj)


def _alibi_slope(h):
    return 2.0 ** (-8.0 * (h + 1) / ATT_HEADS)


def _attn_kernel(q_ref, k_ref, v_ref, o_ref, lse_ref, *, L, Tq, Tk, dil, seg):
    i = pl.program_id(1)
    t0 = i * Tq
    start = pl.multiple_of(jnp.clip(t0 - ATT_HALF, 0, L - Tk), ATT_HALF)
    ri = lax.broadcasted_iota(jnp.int32, (Tq, Tk), 0)
    ci = lax.broadcasted_iota(jnp.int32, (Tq, Tk), 1)
    absrel = jnp.abs(ci - ri + (start - t0))
    if seg < L:
        absrel = jnp.where((ri + t0) // seg == (ci + start) // seg, absrel, L)
    valid = absrel <= ATT_HALF
    dist = absrel.astype(F32) * float(dil)
    lo = lax.broadcasted_iota(jnp.int32, (1, LANES), 1) < ATT_HEAD_DIM
    lane_t = lax.broadcasted_iota(jnp.int32, (Tq, LANES), 1)
    lse_tile = jnp.zeros((Tq, LANES), F32)
    nt_dn = (((1,), (1,)), ((), ()))

    for p in range(ATT_HEADS // 2):
        cs = slice(LANES * p, LANES * (p + 1))
        q2 = q_ref[:, cs]
        k2 = k_ref[pl.ds(start, Tk), cs]
        v2 = v_ref[pl.ds(start, Tk), cs]
        per_head = []
        for hh in range(2):
            sel = lo if hh == 0 else jnp.logical_not(lo)
            qh = jnp.where(sel, q2, jnp.zeros_like(q2))
            s = lax.dot_general(qh, k2, nt_dn, preferred_element_type=F32)
            s = jnp.where(valid, s - dist * _alibi_slope(2 * p + hh), NEG_BIG)
            m = jnp.max(s, axis=-1, keepdims=True)
            pr = jnp.exp(s - m).astype(BF16)
            vext = jnp.where(sel, v2, jnp.ones_like(v2))
            per_head.append((jnp.dot(pr, vext, preferred_element_type=F32), m))
        (of0, m0), (of1, m1) = per_head
        num = jnp.where(lo, of0, of1)
        den = pltpu.roll(jnp.where(lo, of1, of0), ATT_HEAD_DIM, axis=1)
        o_ref[:, cs] = (num / den).astype(o_ref.dtype)
        lse0 = m0 + jnp.log(of0[:, ATT_HEAD_DIM:ATT_HEAD_DIM + 1])
        lse1 = m1 + jnp.log(of1[:, 0:1])
        lse_tile = jnp.where(lane_t == 2 * p, lse0, jnp.where(lane_t == 2 * p + 1, lse1, lse_tile))
    lse_ref[...] = lse_tile


def window_attention(q, k, v, cols, *, dil):
    bp0, seg = q.shape[0], q.shape[1]
    hd = ATT_HEADS * ATT_HEAD_DIM
    stack = max(1, ATT_TQ // seg)
    if stack > 1:
        q, k, v = (a.reshape(bp0 // stack, stack * seg, a.shape[-1]) for a in (q, k, v))
    bp, L = q.shape[0], q.shape[1]
    Tq = min(ATT_TQ, L)
    Tk = min(L, Tq + 2 * ATT_HALF)
    qc, kc, vc = cols
    out, lse = pl.pallas_call(
        functools.partial(_attn_kernel, L=L, Tq=Tq, Tk=Tk, dil=dil, seg=seg),
        out_shape=(jax.ShapeDtypeStruct((bp, L, hd), BF16),
                   jax.ShapeDtypeStruct((bp, L, LANES), F32)),
        grid=(bp, L // Tq),
        in_specs=[
            pl.BlockSpec((None, Tq, hd), lambda b, i: (b, i, qc)),
            pl.BlockSpec((None, L, hd), lambda b, i: (b, 0, kc)),
            pl.BlockSpec((None, L, hd), lambda b, i: (b, 0, vc)),
        ],
        out_specs=(pl.BlockSpec((None, Tq, hd), lambda b, i: (b, i, 0)),
                   pl.BlockSpec((None, Tq, LANES), lambda b, i: (b, i, 0))),
        compiler_params=_cparams(("parallel", "arbitrary")),
        name=f"window_attention_d{dil}",
    )(q, k, v)
    return out.reshape(bp0, seg, hd), lse.reshape(bp0, seg, LANES)


def _merge_kernel(o0_ref, o1_ref, o2_ref, l0_ref, l1_ref, l2_ref, e_ref, w_ref, r_ref, out_ref):
    ls = [l0_ref[...], l1_ref[...], l2_ref[...]]
    m = jnp.maximum(jnp.maximum(ls[0], ls[1]), ls[2])
    es = [jnp.exp(l - m) for l in ls]
    inv = 1.0 / (es[0] + es[1] + es[2])
    acc = None
    for e, o_ref in zip(es, (o0_ref, o1_ref, o2_ref)):
        wt = e * inv
        hi = wt.astype(BF16)
        lo = (wt - hi.astype(F32)).astype(BF16)
        wfull = (jnp.dot(hi, e_ref[...], preferred_element_type=F32)
                 + jnp.dot(lo, e_ref[...], preferred_element_type=F32))
        term = wfull * o_ref[...].astype(F32)
        acc = term if acc is None else acc + term
    out_ref[...] = r_ref[...] + jnp.dot(acc.astype(BF16), w_ref[...], preferred_element_type=F32)


def merge_groups_project(outs, lses, w_out, res, *, tm=1024):
    t, hd = outs[0].shape
    n = w_out.shape[1]
    tm = _row_tile(t, tm)
    hrow = jnp.arange(LANES)[:, None]
    hcol = jnp.arange(hd)[None, :] // ATT_HEAD_DIM
    expand = (hrow == hcol).astype(BF16)
    o_spec = pl.BlockSpec((tm, hd), lambda i: (i, 0))
    l_spec = pl.BlockSpec((tm, LANES), lambda i: (i, 0))
    return pl.pallas_call(
        _merge_kernel,
        out_shape=jax.ShapeDtypeStruct((t, n), F32),
        grid=(t // tm,),
        in_specs=[o_spec, o_spec, o_spec, l_spec, l_spec, l_spec,
                  pl.BlockSpec((LANES, hd), lambda i: (0, 0)),
                  pl.BlockSpec((hd, n), lambda i: (0, 0)),
                  pl.BlockSpec((tm, n), lambda i: (i, 0))],
        out_specs=pl.BlockSpec((tm, n), lambda i: (i, 0)),
        compiler_params=_cparams(("parallel",)),
        name="merge_groups_project",
    )(*outs, *lses, expand, w_out, res)


def _router_kernel(x_ref, g_ref, w_ref, idx_ref, wt_ref):
    hn = _rms_rows(x_ref[...], g_ref[...]).astype(BF16)
    logits = jnp.dot(hn, w_ref[...], preferred_element_type=F32)
    lane = lax.broadcasted_iota(jnp.int32, logits.shape, 1)
    neg_inf = jnp.float32(-jnp.inf)
    l1 = jnp.where(lane < N_EXPERTS, logits, neg_inf)
    m1 = jnp.max(l1, axis=-1, keepdims=True)
    lane_f = lane.astype(F32)
    i1 = jnp.min(jnp.where(l1 == m1, lane_f, float(LANES)), axis=-1, keepdims=True).astype(jnp.int32)
    l2 = jnp.where(lane == i1, neg_inf, l1)
    m2 = jnp.max(l2, axis=-1, keepdims=True)
    i2 = jnp.min(jnp.where(l2 == m2, lane_f, float(LANES)), axis=-1, keepdims=True).astype(jnp.int32)
    e2 = jnp.exp(m2 - m1)
    denom = 1.0 + e2
    idx_ref[...] = jnp.where(lane == 0, i1, jnp.where(lane == 1, i2, 0))
    wt_ref[...] = jnp.where(lane == 0, 1.0 / denom, jnp.where(lane == 1, e2 / denom, 0.0))


def moe_router(x, g, router_w, *, tm=1024):
    t, d = x.shape
    tm = _row_tile(t, tm)
    w = jnp.zeros((d, LANES), BF16).at[:, :N_EXPERTS].set(router_w.astype(BF16))
    return pl.pallas_call(
        _router_kernel,
        out_shape=(jax.ShapeDtypeStruct((t, LANES), jnp.int32),
                   jax.ShapeDtypeStruct((t, LANES), F32)),
        grid=(t // tm,),
        in_specs=[pl.BlockSpec((tm, d), lambda i: (i, 0)),
                  pl.BlockSpec((1, d), lambda i: (0, 0)),
                  pl.BlockSpec((d, LANES), lambda i: (0, 0))],
        out_specs=(pl.BlockSpec((tm, LANES), lambda i: (i, 0)),
                   pl.BlockSpec((tm, LANES), lambda i: (i, 0))),
        compiler_params=_cparams(("parallel",)),
        name="moe_router",
    )(x, g.reshape(1, d), w)


def _row_copy(src, src_row, dst, dst_row, sem):
    return pltpu.make_async_copy(src.at[pl.ds(src_row, 1)], dst.at[pl.ds(dst_row, 1)], sem)


def _scatter_kernel(pos_ref, x_ref, init_ref, xs_hbm, sem, *, tm):
    del init_ref

    def issue(r, carry):
        for k in range(TOP_K):
            _row_copy(x_ref, r, xs_hbm, pos_ref[TOP_K * r + k], sem).start()
        return carry

    lax.fori_loop(0, tm, issue, 0)

    def drain(r, carry):
        for k in range(TOP_K):
            _row_copy(x_ref, 0, xs_hbm, 0, sem).wait()
        return carry

    lax.fori_loop(0, tm, drain, 0)


def moe_scatter_rows(x, pos_flat, n_slots, *, tm=512):
    t, d = x.shape
    tm = _row_tile(t, tm)
    init = jnp.zeros((n_slots, d), x.dtype)
    return pl.pallas_call(
        functools.partial(_scatter_kernel, tm=tm),
        out_shape=jax.ShapeDtypeStruct((n_slots, d), x.dtype),
        grid=(t // tm,),
        in_specs=[pl.BlockSpec((TOP_K * tm,), lambda i: (i,), memory_space=pltpu.SMEM),
                  pl.BlockSpec((tm, d), lambda i: (i, 0)),
                  pl.BlockSpec(memory_space=pl.ANY)],
        out_specs=pl.BlockSpec(memory_space=pl.ANY),
        scratch_shapes=[pltpu.SemaphoreType.DMA(())],
        input_output_aliases={2: 0},
        compiler_params=_cparams(("arbitrary",)),
        name="moe_scatter_rows",
    )(pos_flat, x, init)


def _combine_kernel(pos_ref, h_ref, wt_ref, y_hbm, o_ref, ya_ref, yb_ref, sem, *, tm):
    bufs = (ya_ref, yb_ref)

    def issue(r, carry):
        for k in range(TOP_K):
            _row_copy(y_hbm, pos_ref[TOP_K * r + k], bufs[k], r, sem).start()
        return carry

    lax.fori_loop(0, tm, issue, 0)

    def drain(r, carry):
        for k in range(TOP_K):
            _row_copy(y_hbm, 0, bufs[k], 0, sem).wait()
        return carry

    lax.fori_loop(0, tm, drain, 0)
    wt = wt_ref[...]
    o_ref[...] = h_ref[...] + wt[:, 0:1] * ya_ref[...] + wt[:, 1:2] * yb_ref[...]


def moe_combine(h, wt, y_sorted, pos_flat, *, tm=512):
    t, d = h.shape
    tm = _row_tile(t, tm)
    return pl.pallas_call(
        functools.partial(_combine_kernel, tm=tm),
        out_shape=jax.ShapeDtypeStruct((t, d), F32),
        grid=(t // tm,),
        in_specs=[pl.BlockSpec((TOP_K * tm,), lambda i: (i,), memory_space=pltpu.SMEM),
                  pl.BlockSpec((tm, d), lambda i: (i, 0)),
                  pl.BlockSpec((tm, LANES), lambda i: (i, 0)),
                  pl.BlockSpec(memory_space=pl.ANY)],
        out_specs=pl.BlockSpec((tm, d), lambda i: (i, 0)),
        scratch_shapes=[pltpu.VMEM((tm, d), F32), pltpu.VMEM((tm, d), F32),
                        pltpu.SemaphoreType.DMA(())],
        compiler_params=_cparams(("arbitrary",)),
        name="moe_combine",
    )(pos_flat, h, wt, y_sorted)


def _moe_tile_rows(t):
    return 1024 if t >= 16384 else 256


def moe_ffn(h, norm_g, router_w, w_gate_up, w_down):
    t, d = h.shape
    tm = _moe_tile_rows(t)
    idx, wt = moe_router(h, norm_g, router_w)
    e_flat = idx[:, :TOP_K].reshape(-1)
    onehot = (e_flat[:, None] == jnp.arange(N_EXPERTS)[None, :]).astype(jnp.int32)
    csum = jnp.cumsum(onehot, axis=0)
    rank = jnp.take_along_axis(csum, e_flat[:, None], axis=1)[:, 0] - 1
    counts = csum[-1]
    tiles_per = (counts + tm - 1) // tm
    tile_end = jnp.cumsum(tiles_per)
    starts = (tile_end - tiles_per) * tm
    pos = (starts[e_flat] + rank).astype(jnp.int32)
    n_tiles = (TOP_K * t) // tm + N_EXPERTS
    tile_ids = jnp.arange(n_tiles)
    tile_expert = jnp.minimum(jnp.sum(tile_ids[:, None] >= tile_end[None, :], axis=1),
                              N_EXPERTS - 1).astype(jnp.int32)
    n_used = tile_end[-1:].astype(jnp.int32)

    xs = moe_scatter_rows(h, pos, n_tiles * tm)
    ys = swiglu_ffn(xs, norm_g, w_gate_up, w_down, tile_expert, n_used, tm=tm,
                    tf=_ff_tile(w_down.shape[1]), add_residual=False)
    return moe_combine(h, wt, ys, pos)


def retention_mixer(h, norm_g, w_in, decay_fwd, decay_bwd, gn, w_out, *, batch, seq):
    proj = norm_matmul(h, norm_g, w_in)
    gated = retention_core(proj, decay_fwd, decay_bwd, gn, batch=batch, seq=seq)
    return matmul_residual(gated, w_out, h)


def _to_residue_major(a, batch, seq, dil):
    L = seq // dil
    return a.reshape(batch, L, dil, a.shape[-1]).transpose(0, 2, 1, 3).reshape(batch * dil, L, a.shape[-1])


def _from_residue_major(a, batch, seq, dil):
    L = seq // dil
    return a.reshape(batch, dil, L, a.shape[-1]).transpose(0, 2, 1, 3).reshape(batch * seq, a.shape[-1])


def dilated_attention_mixer(h, norm_g, w_in, q_gain, k_gain, w_out, *, batch, seq):
    hd = ATT_HEADS * ATT_HEAD_DIM
    n_groups = len(ATT_GROUPS)
    proj = att_inproj(h, norm_g, w_in, q_gain, k_gain).reshape(batch, seq, 3 * n_groups * hd)
    outs, lses = [], []
    for g, (_, dil) in enumerate(ATT_GROUPS):
        if dil == 1:
            o, lse = window_attention(proj, proj, proj, (g, n_groups + g, 2 * n_groups + g), dil=1)
        else:
            def pick(c):
                blk = proj[:, :, (c * n_groups + g) * hd:(c * n_groups + g + 1) * hd]
                return _to_residue_major(blk, batch, seq, dil)
            o, lse = window_attention(pick(0), pick(1), pick(2), (0, 0, 0), dil=dil)
        outs.append(_from_residue_major(o, batch, seq, dil))
        lses.append(_from_residue_major(lse, batch, seq, dil))
    return merge_groups_project(outs, lses, w_out, h)


def dense_ffn(h, norm_g, w_gate_up, w_down):
    t = h.shape[0]
    ff = w_down.shape[0]
    zero = jnp.zeros((t // _row_tile(t, 1024),), jnp.int32)
    n_used = jnp.full((1,), zero.shape[0], jnp.int32)
    return swiglu_ffn(h, norm_g, w_gate_up[None], w_down[None], zero, n_used,
                      tm=_row_tile(t, 1024), tf=_ff_tile(ff), add_residual=True)


def run_trunk(x, p, w):
    batch, seq, d = x.shape
    depth = p.shape[0]
    h = x.reshape(batch * seq, d)
    p2 = p.reshape(depth, batch * seq, p.shape[-1])
    for i in range(depth):
        j = i // 2
        if i % 2 == 0:
            h = retention_mixer(h, w['ret_norm'][j], w['ret_w_in'][j], w['ret_decay_fwd'][j],
                                w['ret_decay_bwd'][j], w['ret_gn'][j], w['ret_w_out'][j],
                                batch=batch, seq=seq)
            h = dense_ffn(h, w['ffn_norm'][j], w['ffn_w_gate_up'][j], w['ffn_w_down'][j])
        else:
            h = dilated_attention_mixer(h, w['att_norm'][j], w['att_w_in'][j], w['att_q_gain'][j],
                                        w['att_k_gain'][j], w['att_w_out'][j], batch=batch, seq=seq)
            h = moe_ffn(h, w['moe_norm'][j], w['moe_router'][j], w['moe_w_gate_up'][j],
                        w['moe_w_down'][j])
        h = per_layer_embedding(h, p2[i], w['ple_norm'][i], w['ple_w_gate'][i], w['ple_w_proj'][i])
    return h.reshape(batch, seq, d)


_MATMUL_WEIGHTS = ('ret_w_in', 'ret_w_out', 'att_w_in', 'att_w_out', 'ffn_w_gate_up', 'ffn_w_down',
                   'moe_w_gate_up', 'moe_w_down', 'ple_w_gate', 'ple_w_proj')


def kernel(x_prompt, x_sample, p_prompt, p_sample,
           ret_norm, ret_w_in, ret_decay_fwd, ret_decay_bwd, ret_gn, ret_w_out,
           att_norm, att_w_in, att_q_gain, att_k_gain, att_w_out,
           ffn_norm, ffn_w_gate_up, ffn_w_down,
           moe_norm, moe_router, moe_w_gate_up, moe_w_down,
           ple_norm, ple_w_gate, ple_w_proj):
    w = {
        'ret_norm': ret_norm, 'ret_w_in': ret_w_in, 'ret_decay_fwd': ret_decay_fwd,
        'ret_decay_bwd': ret_decay_bwd, 'ret_gn': ret_gn, 'ret_w_out': ret_w_out,
        'att_norm': att_norm, 'att_w_in': att_w_in, 'att_q_gain': att_q_gain,
        'att_k_gain': att_k_gain, 'att_w_out': att_w_out,
        'ffn_norm': ffn_norm, 'ffn_w_gate_up': ffn_w_gate_up, 'ffn_w_down': ffn_w_down,
        'moe_norm': moe_norm, 'moe_router': moe_router, 'moe_w_gate_up': moe_w_gate_up,
        'moe_w_down': moe_w_down,
        'ple_norm': ple_norm, 'ple_w_gate': ple_w_gate, 'ple_w_proj': ple_w_proj,
    }
    for name in _MATMUL_WEIGHTS:
        w[name] = w[name].astype(BF16)
    nb = x_prompt.shape[0]
    x = jnp.concatenate([x_prompt, x_sample], axis=0)
    p = jnp.concatenate([p_prompt, p_sample], axis=1)
    y = run_trunk(x, p, w)
    return (y[:nb], y[nb:])
```

```python
import functools

import jax
import jax.numpy as jnp
from jax import lax
from jax.experimental import pallas as pl
from jax.experimental.pallas import tpu as pltpu

F32 = jnp.float32
BF16 = jnp.bfloat16

EPS = 1e-6
NEG_BIG = -1e30

RET_CHUNK = 256
ATT_HEADS = 16
ATT_HEAD_DIM = 64
ATT_GROUPS = ((128, 1), (512, 4), (2048, 16))
ATT_HALF = 64
ATT_TQ = 256
N_EXPERTS = 8
TOP_K = 2
LANES = 128
MXU_TILE = 256
VMEM_LIMIT = 56 * 1024 * 1024


def _cparams(sem, vmem=VMEM_LIMIT):
    return pltpu.CompilerParams(dimension_semantics=sem, vmem_limit_bytes=vmem)


def _rms_rows(x, g):
    ms = jnp.mean(x * x, axis=-1, keepdims=True)
    return x * lax.rsqrt(ms + EPS) * g


def _row_tile(t, pref):
    return pref if t % pref == 0 else t


def _ff_tile(ff, pref=512):
    for tf in range(pref, 0, -LANES):
        if ff % tf == 0:
            return tf
    return ff


def _norm_matmul_kernel(x_ref, g_ref, w_ref, o_ref, xs_ref):
    @pl.when(pl.program_id(1) == 0)
    def _():
        xs_ref[...] = _rms_rows(x_ref[...], g_ref[...]).astype(BF16)

    o_ref[...] = jnp.dot(xs_ref[...], w_ref[...], preferred_element_type=F32).astype(o_ref.dtype)


def norm_matmul(x, g, w, *, tm=1024, tn=1024, out_dtype=BF16):
    t, d = x.shape
    n = w.shape[1]
    tm = _row_tile(t, tm)
    return pl.pallas_call(
        _norm_matmul_kernel,
        out_shape=jax.ShapeDtypeStruct((t, n), out_dtype),
        grid=(t // tm, n // tn),
        in_specs=[
            pl.BlockSpec((tm, d), lambda i, j: (i, 0)),
            pl.BlockSpec((1, d), lambda i, j: (0, 0)),
            pl.BlockSpec((d, tn), lambda i, j: (0, j)),
        ],
        out_specs=pl.BlockSpec((tm, tn), lambda i, j: (i, j)),
        scratch_shapes=[pltpu.VMEM((tm, d), BF16)],
        compiler_params=_cparams(("parallel", "arbitrary")),
        name="norm_matmul",
    )(x, g.reshape(1, d), w)


def _att_inproj_kernel(x_ref, g_ref, w_ref, gain_ref, bd_ref, o_ref, xs_ref, *, n_qk_blocks):
    j = pl.program_id(1)

    @pl.when(j == 0)
    def _():
        xs_ref[...] = _rms_rows(x_ref[...], g_ref[...]).astype(BF16)

    y = jnp.dot(xs_ref[...], w_ref[...], preferred_element_type=F32)
    tn = y.shape[1]

    @pl.when(j < n_qk_blocks)
    def _():
        for c in range(tn // MXU_TILE):
            cs = slice(c * MXU_TILE, (c + 1) * MXU_TILE)
            yc = y[:, cs]
            ss = jnp.dot((yc * yc).astype(BF16), bd_ref[...], preferred_element_type=F32)
            yn = yc * lax.rsqrt(ss * (1.0 / ATT_HEAD_DIM) + EPS) * gain_ref[:, cs]
            o_ref[:, cs] = yn.astype(o_ref.dtype)

    @pl.when(j >= n_qk_blocks)
    def _():
        o_ref[...] = y.astype(o_ref.dtype)


def att_inproj(x, g, w, q_gain, k_gain, *, tm=1024):
    t, d = x.shape
    n = w.shape[1]
    tn = ATT_HEADS * ATT_HEAD_DIM
    n_groups = len(ATT_GROUPS)
    tm = _row_tile(t, tm)
    gains = jnp.stack([jnp.tile(q_gain, ATT_HEADS) * (ATT_HEAD_DIM ** -0.5),
                       jnp.tile(k_gain, ATT_HEADS)]).reshape(2, 1, tn).astype(F32)
    r = jnp.arange(MXU_TILE) // ATT_HEAD_DIM
    bd = (r[:, None] == r[None, :]).astype(BF16)
    return pl.pallas_call(
        functools.partial(_att_inproj_kernel, n_qk_blocks=2 * n_groups),
        out_shape=jax.ShapeDtypeStruct((t, n), BF16),
        grid=(t // tm, n // tn),
        in_specs=[
            pl.BlockSpec((tm, d), lambda i, j: (i, 0)),
            pl.BlockSpec((1, d), lambda i, j: (0, 0)),
            pl.BlockSpec((d, tn), lambda i, j: (0, j)),
            pl.BlockSpec((None, 1, tn), lambda i, j: (jnp.minimum(j // n_groups, 1), 0, 0)),
            pl.BlockSpec((MXU_TILE, MXU_TILE), lambda i, j: (0, 0)),
        ],
        out_specs=pl.BlockSpec((tm, tn), lambda i, j: (i, j)),
        scratch_shapes=[pltpu.VMEM((tm, d), BF16)],
        compiler_params=_cparams(("parallel", "arbitrary")),
        name="att_inproj",
    )(x, g.reshape(1, d), w, gains, bd)


def _retention_kernel(dec_ref, q_ref, k_ref, v_ref, g_ref, gn_ref, o_ref,
                      state_ref, bc_ref, dmat_ref, vec_ref, *, nc, C, H, dk, dv):
    s = pl.program_id(1)
    qscale = dk ** -0.5
    tn_dn = (((0,), (0,)), ((), ()))
    nt_dn = (((1,), (1,)), ((), ()))

    @pl.when(s == 0)
    def _tables():
        row = lax.broadcasted_iota(jnp.int32, (C, C), 0)
        col = lax.broadcasted_iota(jnp.int32, (C, C), 1)
        diff = (row - col).astype(F32)
        ridx = lax.broadcasted_iota(jnp.int32, (C, LANES), 0).astype(F32)
        for h in range(H):
            lf = dec_ref[0, h]
            lb = dec_ref[1, h]
            dmat_ref[h] = jnp.exp(jnp.where(diff >= 0, lf, -lb) * diff) * qscale
            vec_ref[0, h] = jnp.exp(lf * (ridx + 1.0)) * qscale
            vec_ref[1, h] = jnp.exp(lf * (C - 1.0 - ridx))
            vec_ref[2, h] = jnp.exp(lb * (C - ridx)) * qscale
            vec_ref[3, h] = jnp.exp(lb * ridx)

    @pl.when((s == 0) | (s == nc))
    def _zero():
        state_ref[...] = jnp.zeros_like(state_ref)

    def head_slices(h):
        return slice(h * dk, (h + 1) * dk), slice(h * dv, (h + 1) * dv)

    def state_step(h, kh, vh, st, zeta, chunk_decay):
        kz = (kh.astype(F32) * jnp.tile(zeta, (1, dk // LANES))).astype(BF16)
        upd = lax.dot_general(kz, vh, tn_dn, preferred_element_type=F32)
        state_ref[h] = st * chunk_decay + upd

    @pl.when(s < nc)
    def _backward():
        c = nc - 1 - s
        for h in range(H):
            ks, vs = head_slices(h)
            qh, kh, vh = q_ref[:, ks], k_ref[:, ks], v_ref[:, vs]
            st = state_ref[h]
            cross = jnp.dot(qh, st.astype(BF16), preferred_element_type=F32)
            bc_ref[c, :, vs] = cross * jnp.tile(vec_ref[2, h], (1, dv // LANES))
            state_step(h, kh, vh, st, vec_ref[3, h], dec_ref[3, h])

    @pl.when(s >= nc)
    def _forward():
        c = s - nc
        for h in range(H):
            ks, vs = head_slices(h)
            qh, kh, vh = q_ref[:, ks], k_ref[:, ks], v_ref[:, vs]
            st = state_ref[h]
            sr = lax.dot_general(qh, kh, nt_dn, preferred_element_type=F32)
            intra = jnp.dot((sr * dmat_ref[h]).astype(BF16), vh, preferred_element_type=F32)
            cross = jnp.dot(qh, st.astype(BF16), preferred_element_type=F32)
            o = intra + cross * jnp.tile(vec_ref[0, h], (1, dv // LANES)) + bc_ref[c, :, vs]
            state_step(h, kh, vh, st, vec_ref[1, h], dec_ref[2, h])
            on = _rms_rows(o, gn_ref[:, vs])
            gg = g_ref[:, vs].astype(F32)
            o_ref[:, vs] = (on * (gg * jax.nn.sigmoid(gg))).astype(o_ref.dtype)


def retention_core(proj, decay_fwd, decay_bwd, gn, *, batch, seq):
    H = decay_fwd.shape[0]
    dv = gn.shape[1]
    dk = dv // 2
    C = RET_CHUNK
    nc = seq // C
    qk_w, v_w = H * dk, H * dv
    proj3 = proj.reshape(batch, seq, 2 * qk_w + 2 * v_w)
    lg_f = jnp.log1p(-jnp.exp(decay_fwd.astype(F32)))
    lg_b = jnp.log1p(-jnp.exp(decay_bwd.astype(F32)))
    dec = jnp.stack([lg_f, lg_b, jnp.exp(lg_f * C), jnp.exp(lg_b * C)])

    def cidx(s):
        return jnp.where(s < nc, nc - 1 - s, s - nc)

    def oidx(s):
        return jnp.maximum(s - nc, 0)

    grid_spec = pltpu.PrefetchScalarGridSpec(
        num_scalar_prefetch=1,
        grid=(batch, 2 * nc),
        in_specs=[
            pl.BlockSpec((None, C, qk_w), lambda b, s, d: (b, cidx(s), 0)),
            pl.BlockSpec((None, C, qk_w), lambda b, s, d: (b, cidx(s), 1)),
            pl.BlockSpec((None, C, v_w), lambda b, s, d: (b, cidx(s), 2 * qk_w // v_w)),
            pl.BlockSpec((None, C, v_w), lambda b, s, d: (b, oidx(s), (2 * qk_w + v_w) // v_w)),
            pl.BlockSpec((1, v_w), lambda b, s, d: (0, 0)),
        ],
        out_specs=pl.BlockSpec((None, C, v_w), lambda b, s, d: (b, oidx(s), 0)),
        scratch_shapes=[
            pltpu.VMEM((H, dk, dv), F32),
            pltpu.VMEM((nc, C, v_w), F32),
            pltpu.VMEM((H, C, C), F32),
            pltpu.VMEM((4, H, C, LANES), F32),
        ],
    )
    out = pl.pallas_call(
        functools.partial(_retention_kernel, nc=nc, C=C, H=H, dk=dk, dv=dv),
        out_shape=jax.ShapeDtypeStruct((batch, seq, v_w), BF16),
        grid_spec=grid_spec,
        compiler_params=_cparams(("arbitrary", "arbitrary")),
        name="retention",
    )(dec, proj3, proj3, proj3, proj3, gn.reshape(1, v_w).astype(F32))
    return out.reshape(batch * seq, v_w)


def _matmul_res_kernel(a_ref, w_ref, r_ref, o_ref):
    o_ref[...] = r_ref[...] + jnp.dot(a_ref[...], w_ref[...], preferred_element_type=F32)


def matmul_residual(a, w, res, *, tm=1024):
    t, k = a.shape
    n = w.shape[1]
    tm = _row_tile(t, tm)
    return pl.pallas_call(
        _matmul_res_kernel,
        out_shape=jax.ShapeDtypeStruct((t, n), F32),
        grid=(t // tm,),
        in_specs=[
            pl.BlockSpec((tm, k), lambda i: (i, 0)),
            pl.BlockSpec((k, n), lambda i: (0, 0)),
            pl.BlockSpec((tm, n), lambda i: (i, 0)),
        ],
        out_specs=pl.BlockSpec((tm, n), lambda i: (i, 0)),
        compiler_params=_cparams(("parallel",)),
        name="matmul_residual",
    )(a, w, res)


def _ffn_kernel(eid_ref, nused_ref, x_ref, g_ref, wg_ref, wu_ref, wd_ref, *rest, n_f, add_residual,
                fuse_ple):
    if fuse_ple:
        p_ref, pg_ref, pwg_ref, pwp_ref, o_ref, xs_ref, acc_ref = rest
    else:
        o_ref, xs_ref, acc_ref = rest
    i = pl.program_id(0)
    f = pl.program_id(1)
    used = i < nused_ref[0]

    @pl.when(used & (f == 0))
    def _():
        xs_ref[...] = _rms_rows(x_ref[...], g_ref[...]).astype(BF16)
        acc_ref[...] = jnp.zeros_like(acc_ref)

    @pl.when(used)
    def _():
        xs = xs_ref[...]
        a = jnp.dot(xs, wg_ref[...], preferred_element_type=F32)
        u = jnp.dot(xs, wu_ref[...], preferred_element_type=F32)
        hid = (a * jax.nn.sigmoid(a) * u).astype(BF16)
        acc_ref[...] += jnp.dot(hid, wd_ref[...], preferred_element_type=F32)

    @pl.when(used & (f == n_f - 1))
    def _():
        if fuse_ple:
            h1 = x_ref[...] + acc_ref[...]
            hn = _rms_rows(h1, pg_ref[...]).astype(BF16)
            gate = jax.nn.sigmoid(jnp.dot(hn, pwg_ref[...], preferred_element_type=F32))
            emb = jnp.dot(p_ref[...].astype(BF16), pwp_ref[...], preferred_element_type=F32)
            o_ref[...] = h1 + gate * emb
        elif add_residual:
            o_ref[...] = x_ref[...] + acc_ref[...]
        else:
            o_ref[...] = acc_ref[...]

    @pl.when(jnp.logical_not(used) & (f == n_f - 1))
    def _():
        o_ref[...] = jnp.zeros_like(o_ref)


def swiglu_ffn(x, g, w_gate_up, w_down, tile_expert, n_used, *, tm, tf, add_residual, ple=None):
    t, d = x.shape
    ff = w_down.shape[1]
    n_f = ff // tf

    def fe(i, f, nu):
        return jnp.where(i < nu[0], f, 0)

    in_specs = [
        pl.BlockSpec((tm, d), lambda i, f, e, nu: (i, 0)),
        pl.BlockSpec((1, d), lambda i, f, e, nu: (0, 0)),
        pl.BlockSpec((None, d, tf), lambda i, f, e, nu: (e[i], 0, fe(i, f, nu))),
        pl.BlockSpec((None, d, tf), lambda i, f, e, nu: (e[i], 0, n_f + fe(i, f, nu))),
        pl.BlockSpec((None, tf, d), lambda i, f, e, nu: (e[i], fe(i, f, nu), 0)),
    ]
    args = [x, g.reshape(1, d), w_gate_up, w_gate_up, w_down]
    if ple is not None:
        p, pg, pwg, pwp = ple
        pd = p.shape[1]
        in_specs += [
            pl.BlockSpec((tm, pd), lambda i, f, e, nu: (i, 0)),
            pl.BlockSpec((1, d), lambda i, f, e, nu: (0, 0)),
            pl.BlockSpec((d, d), lambda i, f, e, nu: (0, 0)),
            pl.BlockSpec((pd, d), lambda i, f, e, nu: (0, 0)),
        ]
        args += [p, pg.reshape(1, d), pwg, pwp]
    grid_spec = pltpu.PrefetchScalarGridSpec(
        num_scalar_prefetch=2,
        grid=(t // tm, n_f),
        in_specs=in_specs,
        out_specs=pl.BlockSpec((tm, d), lambda i, f, e, nu: (i, 0)),
        scratch_shapes=[pltpu.VMEM((tm, d), BF16), pltpu.VMEM((tm, d), F32)],
    )
    return pl.pallas_call(
        functools.partial(_ffn_kernel, n_f=n_f, add_residual=add_residual, fuse_ple=ple is not None),
        out_shape=jax.ShapeDtypeStruct((t, d), F32),
        grid_spec=grid_spec,
        compiler_params=_cparams(("parallel", "arbitrary")),
        name="swiglu_ffn_ple" if ple is not None else "swiglu_ffn",
    )(tile_expert, n_used, *args)


def _ple_kernel(h_ref, p_ref, g_ref, wg_ref, wp_ref, o_ref):
    h = h_ref[...]
    hn = _rms_rows(h, g_ref[...]).astype(BF16)
    gate = jax.nn.sigmoid(jnp.dot(hn, wg_ref[...], preferred_element_type=F32))
    emb = jnp.dot(p_ref[...].astype(BF16), wp_ref[...], preferred_element_type=F32)
    o_ref[...] = h + gate * emb


def per_layer_embedding(h, p, g, w_gate, w_proj, *, tm=1024):
    t, d = h.shape
    pd = p.shape[1]
    tm = _row_tile(t, tm)
    return pl.pallas_call(
        _ple_kernel,
        out_shape=jax.ShapeDtypeStruct((t, d), F32),
        grid=(t // tm,),
        in_specs=[
            pl.BlockSpec((tm, d), lambda i: (i, 0)),
            pl.BlockSpec((tm, pd), lambda i: (i, 0)),
            pl.BlockSpec((1, d), lambda i: (0, 0)),
            pl.BlockSpec((d, d), lambda i: (0, 0)),
            pl.BlockSpec((pd, d), lambda i: (0, 0)),
        ],
        out_specs=pl.BlockSpec((tm, d), lambda i: (i, 0)),
        compiler_params=_cparams(("parallel",)),
        name="per_layer_embedding",
    )(h, p, g.reshape(1, d), w_gate, w_proj)


def _alibi_slope(h):
    return 2.0 ** (-8.0 * (h + 1) / ATT_HEADS)


def _attn_kernel(q_ref, k_ref, v_ref, o_ref, lse_ref, *, L, Tq, Tk, dil, seg):
    i = pl.program_id(1)
    t0 = i * Tq
    start = pl.multiple_of(jnp.clip(t0 - ATT_HALF, 0, L - Tk), ATT_HALF)
    ri = lax.broadcasted_iota(jnp.int32, (Tq, Tk), 0)
    ci = lax.broadcasted_iota(jnp.int32, (Tq, Tk), 1)
    absrel = jnp.abs(ci - ri + (start - t0))
    if seg < L:
        absrel = jnp.where((ri + t0) // seg == (ci + start) // seg, absrel, L)
    valid = absrel <= ATT_HALF
    dist = absrel.astype(F32) * float(dil)
    lo = lax.broadcasted_iota(jnp.int32, (1, LANES), 1) < ATT_HEAD_DIM
    lane_t = lax.broadcasted_iota(jnp.int32, (Tq, LANES), 1)
    lse_tile = jnp.zeros((Tq, LANES), F32)
    nt_dn = (((1,), (1,)), ((), ()))

    for p in range(ATT_HEADS // 2):
        cs = slice(LANES * p, LANES * (p + 1))
        q2 = q_ref[:, cs]
        k2 = k_ref[pl.ds(start, Tk), cs]
        v2 = v_ref[pl.ds(start, Tk), cs]
        per_head = []
        for hh in range(2):
            sel = lo if hh == 0 else jnp.logical_not(lo)
            qh = jnp.where(sel, q2, jnp.zeros_like(q2))
            s = lax.dot_general(qh, k2, nt_dn, preferred_element_type=F32)
            s = jnp.where(valid, s - dist * _alibi_slope(2 * p + hh), NEG_BIG)
            m = jnp.max(s, axis=-1, keepdims=True)
            pr = jnp.exp(s - m).astype(BF16)
            vext = jnp.where(sel, v2, jnp.ones_like(v2))
            per_head.append((jnp.dot(pr, vext, preferred_element_type=F32), m))
        (of0, m0), (of1, m1) = per_head
        num = jnp.where(lo, of0, of1)
        den = pltpu.roll(jnp.where(lo, of1, of0), ATT_HEAD_DIM, axis=1)
        o_ref[:, cs] = (num / den).astype(o_ref.dtype)
        lse0 = m0 + jnp.log(of0[:, ATT_HEAD_DIM:ATT_HEAD_DIM + 1])
        lse1 = m1 + jnp.log(of1[:, 0:1])
        lse_tile = jnp.where(lane_t == 2 * p, lse0, jnp.where(lane_t == 2 * p + 1, lse1, lse_tile))
    lse_ref[...] = lse_tile


def window_attention(q, k, v, cols, *, dil):
    bp0, seg = q.shape[0], q.shape[1]
    hd = ATT_HEADS * ATT_HEAD_DIM
    stack = max(1, ATT_TQ // seg)
    if stack > 1:
        q, k, v = (a.reshape(bp0 // stack, stack * seg, a.shape[-1]) for a in (q, k, v))
    bp, L = q.shape[0], q.shape[1]
    Tq = min(ATT_TQ, L)
    Tk = min(L, Tq + 2 * ATT_HALF)
    qc, kc, vc = cols
    out, lse = pl.pallas_call(
        functools.partial(_attn_kernel, L=L, Tq=Tq, Tk=Tk, dil=dil, seg=seg),
        out_shape=(jax.ShapeDtypeStruct((bp, L, hd), BF16),
                   jax.ShapeDtypeStruct((bp, L, LANES), F32)),
        grid=(bp, L // Tq),
        in_specs=[
            pl.BlockSpec((None, Tq, hd), lambda b, i: (b, i, qc)),
            pl.BlockSpec((None, L, hd), lambda b, i: (b, 0, kc)),
            pl.BlockSpec((None, L, hd), lambda b, i: (b, 0, vc)),
        ],
        out_specs=(pl.BlockSpec((None, Tq, hd), lambda b, i: (b, i, 0)),
                   pl.BlockSpec((None, Tq, LANES), lambda b, i: (b, i, 0))),
        compiler_params=_cparams(("parallel", "arbitrary")),
        name=f"window_attention_d{dil}",
    )(q, k, v)
    return out.reshape(bp0, seg, hd), lse.reshape(bp0, seg, LANES)


def _merge_kernel(o0_ref, o1_ref, o2_ref, l0_ref, l1_ref, l2_ref, e_ref, w_ref, r_ref, out_ref):
    ls = [l0_ref[...], l1_ref[...], l2_ref[...]]
    m = jnp.maximum(jnp.maximum(ls[0], ls[1]), ls[2])
    es = [jnp.exp(l - m) for l in ls]
    inv = 1.0 / (es[0] + es[1] + es[2])
    acc = None
    for e, o_ref in zip(es, (o0_ref, o1_ref, o2_ref)):
        wt = e * inv
        hi = wt.astype(BF16)
        lo = (wt - hi.astype(F32)).astype(BF16)
        wfull = (jnp.dot(hi, e_ref[...], preferred_element_type=F32)
                 + jnp.dot(lo, e_ref[...], preferred_element_type=F32))
        term = wfull * o_ref[...].astype(F32)
        acc = term if acc is None else acc + term
    out_ref[...] = r_ref[...] + jnp.dot(acc.astype(BF16), w_ref[...], preferred_element_type=F32)


def merge_groups_project(outs, lses, w_out, res, *, tm=1024):
    t, hd = outs[0].shape
    n = w_out.shape[1]
    tm = _row_tile(t, tm)
    hrow = jnp.arange(LANES)[:, None]
    hcol = jnp.arange(hd)[None, :] // ATT_HEAD_DIM
    expand = (hrow == hcol).astype(BF16)
    o_spec = pl.BlockSpec((tm, hd), lambda i: (i, 0))
    l_spec = pl.BlockSpec((tm, LANES), lambda i: (i, 0))
    return pl.pallas_call(
        _merge_kernel,
        out_shape=jax.ShapeDtypeStruct((t, n), F32),
        grid=(t // tm,),
        in_specs=[o_spec, o_spec, o_spec, l_spec, l_spec, l_spec,
                  pl.BlockSpec((LANES, hd), lambda i: (0, 0)),
                  pl.BlockSpec((hd, n), lambda i: (0, 0)),
                  pl.BlockSpec((tm, n), lambda i: (i, 0))],
        out_specs=pl.BlockSpec((tm, n), lambda i: (i, 0)),
        compiler_params=_cparams(("parallel",)),
        name="merge_groups_project",
    )(*outs, *lses, expand, w_out, res)


def _router_kernel(x_ref, g_ref, w_ref, idx_ref, wt_ref):
    hn = _rms_rows(x_ref[...], g_ref[...]).astype(BF16)
    logits = jnp.dot(hn, w_ref[...], preferred_element_type=F32)
    lane = lax.broadcasted_iota(jnp.int32, logits.shape, 1)
    neg_inf = jnp.float32(-jnp.inf)
    l1 = jnp.where(lane < N_EXPERTS, logits, neg_inf)
    m1 = jnp.max(l1, axis=-1, keepdims=True)
    lane_f = lane.astype(F32)
    i1 = jnp.min(jnp.where(l1 == m1, lane_f, float(LANES)), axis=-1, keepdims=True).astype(jnp.int32)
    l2 = jnp.where(lane == i1, neg_inf, l1)
    m2 = jnp.max(l2, axis=-1, keepdims=True)
    i2 = jnp.min(jnp.where(l2 == m2, lane_f, float(LANES)), axis=-1, keepdims=True).astype(jnp.int32)
    e2 = jnp.exp(m2 - m1)
    denom = 1.0 + e2
    idx_ref[...] = jnp.where(lane == 0, i1, jnp.where(lane == 1, i2, 0))
    wt_ref[...] = jnp.where(lane == 0, 1.0 / denom, jnp.where(lane == 1, e2 / denom, 0.0))


def moe_router(x, g, router_w, *, tm=1024):
    t, d = x.shape
    tm = _row_tile(t, tm)
    w = jnp.zeros((d, LANES), BF16).at[:, :N_EXPERTS].set(router_w.astype(BF16))
    return pl.pallas_call(
        _router_kernel,
        out_shape=(jax.ShapeDtypeStruct((t, LANES), jnp.int32),
                   jax.ShapeDtypeStruct((t, LANES), F32)),
        grid=(t // tm,),
        in_specs=[pl.BlockSpec((tm, d), lambda i: (i, 0)),
                  pl.BlockSpec((1, d), lambda i: (0, 0)),
                  pl.BlockSpec((d, LANES), lambda i: (0, 0))],
        out_specs=(pl.BlockSpec((tm, LANES), lambda i: (i, 0)),
                   pl.BlockSpec((tm, LANES), lambda i: (i, 0))),
        compiler_params=_cparams(("parallel",)),
        name="moe_router",
    )(x, g.reshape(1, d), w)


def _row_copy(src, src_row, dst, dst_row, sem):
    return pltpu.make_async_copy(src.at[pl.ds(src_row, 1)], dst.at[pl.ds(dst_row, 1)], sem)


def _scatter_kernel(pos_ref, x_ref, init_ref, xs_hbm, sem, *, tm):
    del init_ref

    def issue(r, carry):
        for k in range(TOP_K):
            _row_copy(x_ref, r, xs_hbm, pos_ref[TOP_K * r + k], sem).start()
        return carry

    lax.fori_loop(0, tm, issue, 0)

    def drain(r, carry):
        for k in range(TOP_K):
            _row_copy(x_ref, 0, xs_hbm, 0, sem).wait()
        return carry

    lax.fori_loop(0, tm, drain, 0)


def moe_scatter_rows(x, pos_flat, n_slots, *, tm=512):
    t, d = x.shape
    tm = _row_tile(t, tm)
    init = jnp.zeros((n_slots, d), x.dtype)
    return pl.pallas_call(
        functools.partial(_scatter_kernel, tm=tm),
        out_shape=jax.ShapeDtypeStruct((n_slots, d), x.dtype),
        grid=(t // tm,),
        in_specs=[pl.BlockSpec((TOP_K * tm,), lambda i: (i,), memory_space=pltpu.SMEM),
                  pl.BlockSpec((tm, d), lambda i: (i, 0)),
                  pl.BlockSpec(memory_space=pl.ANY)],
        out_specs=pl.BlockSpec(memory_space=pl.ANY),
        scratch_shapes=[pltpu.SemaphoreType.DMA(())],
        input_output_aliases={2: 0},
        compiler_params=_cparams(("arbitrary",)),
        name="moe_scatter_rows",
    )(pos_flat, x, init)


def _combine_kernel(pos_ref, h_ref, wt_ref, y_hbm, o_ref, ya_ref, yb_ref, sem, *, tm):
    bufs = (ya_ref, yb_ref)

    def issue(r, carry):
        for k in range(TOP_K):
            _row_copy(y_hbm, pos_ref[TOP_K * r + k], bufs[k], r, sem).start()
        return carry

    lax.fori_loop(0, tm, issue, 0)

    def drain(r, carry):
        for k in range(TOP_K):
            _row_copy(y_hbm, 0, bufs[k], 0, sem).wait()
        return carry

    lax.fori_loop(0, tm, drain, 0)
    wt = wt_ref[...]
    o_ref[...] = h_ref[...] + wt[:, 0:1] * ya_ref[...] + wt[:, 1:2] * yb_ref[...]


def moe_combine(h, wt, y_sorted, pos_flat, *, tm=512):
    t, d = h.shape
    tm = _row_tile(t, tm)
    return pl.pallas_call(
        functools.partial(_combine_kernel, tm=tm),
        out_shape=jax.ShapeDtypeStruct((t, d), F32),
        grid=(t // tm,),
        in_specs=[pl.BlockSpec((TOP_K * tm,), lambda i: (i,), memory_space=pltpu.SMEM),
                  pl.BlockSpec((tm, d), lambda i: (i, 0)),
                  pl.BlockSpec((tm, LANES), lambda i: (i, 0)),
                  pl.BlockSpec(memory_space=pl.ANY)],
        out_specs=pl.BlockSpec((tm, d), lambda i: (i, 0)),
        scratch_shapes=[pltpu.VMEM((tm, d), F32), pltpu.VMEM((tm, d), F32),
                        pltpu.SemaphoreType.DMA(())],
        compiler_params=_cparams(("arbitrary",)),
        name="moe_combine",
    )(pos_flat, h, wt, y_sorted)


def _moe_tile_rows(t):
    return 1024 if t >= 16384 else 256


def moe_ffn(h, norm_g, router_w, w_gate_up, w_down):
    t, d = h.shape
    tm = _moe_tile_rows(t)
    idx, wt = moe_router(h, norm_g, router_w)
    e_flat = idx[:, :TOP_K].reshape(-1)
    onehot = (e_flat[:, None] == jnp.arange(N_EXPERTS)[None, :]).astype(jnp.int32)
    csum = jnp.cumsum(onehot, axis=0)
    rank = jnp.take_along_axis(csum, e_flat[:, None], axis=1)[:, 0] - 1
    counts = csum[-1]
    tiles_per = (counts + tm - 1) // tm
    tile_end = jnp.cumsum(tiles_per)
    starts = (tile_end - tiles_per) * tm
    pos = (starts[e_flat] + rank).astype(jnp.int32)
    n_tiles = (TOP_K * t) // tm + N_EXPERTS
    tile_ids = jnp.arange(n_tiles)
    tile_expert = jnp.minimum(jnp.sum(tile_ids[:, None] >= tile_end[None, :], axis=1),
                              N_EXPERTS - 1).astype(jnp.int32)
    n_used = tile_end[-1:].astype(jnp.int32)

    xs = moe_scatter_rows(h, pos, n_tiles * tm)
    ys = swiglu_ffn(xs, norm_g, w_gate_up, w_down, tile_expert, n_used, tm=tm,
                    tf=_ff_tile(w_down.shape[1]), add_residual=False)
    return moe_combine(h, wt, ys, pos)


def retention_mixer(h, norm_g, w_in, decay_fwd, decay_bwd, gn, w_out, *, batch, seq):
    proj = norm_matmul(h, norm_g, w_in)
    gated = retention_core(proj, decay_fwd, decay_bwd, gn, batch=batch, seq=seq)
    return matmul_residual(gated, w_out, h)


def _to_residue_major(a, batch, seq, dil):
    L = seq // dil
    return a.reshape(batch, L, dil, a.shape[-1]).transpose(0, 2, 1, 3).reshape(batch * dil, L, a.shape[-1])


def _from_residue_major(a, batch, seq, dil):
    L = seq // dil
    return a.reshape(batch, dil, L, a.shape[-1]).transpose(0, 2, 1, 3).reshape(batch * seq, a.shape[-1])


def dilated_attention_mixer(h, norm_g, w_in, q_gain, k_gain, w_out, *, batch, seq):
    hd = ATT_HEADS * ATT_HEAD_DIM
    n_groups = len(ATT_GROUPS)
    proj = att_inproj(h, norm_g, w_in, q_gain, k_gain).reshape(batch, seq, 3 * n_groups * hd)
    outs, lses = [], []
    for g, (_, dil) in enumerate(ATT_GROUPS):
        if dil == 1:
            o, lse = window_attention(proj, proj, proj, (g, n_groups + g, 2 * n_groups + g), dil=1)
        else:
            def pick(c):
                blk = proj[:, :, (c * n_groups + g) * hd:(c * n_groups + g + 1) * hd]
                return _to_residue_major(blk, batch, seq, dil)
            o, lse = window_attention(pick(0), pick(1), pick(2), (0, 0, 0), dil=dil)
        outs.append(_from_residue_major(o, batch, seq, dil))
        lses.append(_from_residue_major(lse, batch, seq, dil))
    return merge_groups_project(outs, lses, w_out, h)


def dense_ffn(h, norm_g, w_gate_up, w_down, ple=None):
    t = h.shape[0]
    ff = w_down.shape[0]
    zero = jnp.zeros((t // _row_tile(t, 1024),), jnp.int32)
    n_used = jnp.full((1,), zero.shape[0], jnp.int32)
    return swiglu_ffn(h, norm_g, w_gate_up[None], w_down[None], zero, n_used,
                      tm=_row_tile(t, 1024), tf=_ff_tile(ff), add_residual=True, ple=ple)


def run_trunk(x, p, w):
    batch, seq, d = x.shape
    depth = p.shape[0]
    h = x.reshape(batch * seq, d)
    p2 = p.reshape(depth, batch * seq, p.shape[-1])
    for i in range(depth):
        j = i // 2
        if i % 2 == 0:
            h = retention_mixer(h, w['ret_norm'][j], w['ret_w_in'][j], w['ret_decay_fwd'][j],
                                w['ret_decay_bwd'][j], w['ret_gn'][j], w['ret_w_out'][j],
                                batch=batch, seq=seq)
            h = dense_ffn(h, w['ffn_norm'][j], w['ffn_w_gate_up'][j], w['ffn_w_down'][j],
                          ple=(p2[i], w['ple_norm'][i], w['ple_w_gate'][i], w['ple_w_proj'][i]))
        else:
            h = dilated_attention_mixer(h, w['att_norm'][j], w['att_w_in'][j], w['att_q_gain'][j],
                                        w['att_k_gain'][j], w['att_w_out'][j], batch=batch, seq=seq)
            h = moe_ffn(h, w['moe_norm'][j], w['moe_router'][j], w['moe_w_gate_up'][j],
                        w['moe_w_down'][j])
            h = per_layer_embedding(h, p2[i], w['ple_norm'][i], w['ple_w_gate'][i], w['ple_w_proj'][i])
    return h.reshape(batch, seq, d)


_MATMUL_WEIGHTS = ('ret_w_in', 'ret_w_out', 'att_w_in', 'att_w_out', 'ffn_w_gate_up', 'ffn_w_down',
                   'moe_w_gate_up', 'moe_w_down', 'ple_w_gate', 'ple_w_proj')


def kernel(x_prompt, x_sample, p_prompt, p_sample,
           ret_norm, ret_w_in, ret_decay_fwd, ret_decay_bwd, ret_gn, ret_w_out,
           att_norm, att_w_in, att_q_gain, att_k_gain, att_w_out,
           ffn_norm, ffn_w_gate_up, ffn_w_down,
           moe_norm, moe_router, moe_w_gate_up, moe_w_down,
           ple_norm, ple_w_gate, ple_w_proj):
    w = {
        'ret_norm': ret_norm, 'ret_w_in': ret_w_in, 'ret_decay_fwd': ret_decay_fwd,
        'ret_decay_bwd': ret_decay_bwd, 'ret_gn': ret_gn, 'ret_w_out': ret_w_out,
        'att_norm': att_norm, 'att_w_in': att_w_in, 'att_q_gain': att_q_gain,
        'att_k_gain': att_k_gain, 'att_w_out': att_w_out,
        'ffn_norm': ffn_norm, 'ffn_w_gate_up': ffn_w_gate_up, 'ffn_w_down': ffn_w_down,
        'moe_norm': moe_norm, 'moe_router': moe_router, 'moe_w_gate_up': moe_w_gate_up,
        'moe_w_down': moe_w_down,
        'ple_norm': ple_norm, 'ple_w_gate': ple_w_gate, 'ple_w_proj': ple_w_proj,
    }
    for name in _MATMUL_WEIGHTS:
        w[name] = w[name].astype(BF16)
    nb = x_prompt.shape[0]
    x = jnp.concatenate([x_prompt, x_sample], axis=0)
    p = jnp.concatenate([p_prompt, p_sample], axis=1)
    y = run_trunk(x, p, w)
    return (y[:nb], y[nb:])
```
